```python
import jax
import jax.numpy as jnp
from jax import lax
import numpy as np


D_MODEL = 1024
BATCH = 16
SEQ = 2048
DEPTH = 4

HEAD_DIM = 64
RWKV_HEADS = 8
RWKV_DIM = RWKV_HEADS * HEAD_DIM
DECAY_LORA = 64
ICLR_LORA = 64
GATE_LORA = 128
GA_Q_HEADS = 8
GA_KV_HEADS = 2
WA_Q_HEADS = 8
WA_KV_HEADS = 2
WINDOW = 128
BLOCK = 128
GRID_W = 64
ROPE_THETA = 10000.0
D_FF = 2816
N_BRANCH = 3
NORM_EPS = 1e-6
LNX_EPS = HEAD_DIM * 1e-5
NEG_INF = -1e30

RWKV_SPLITS = (RWKV_DIM, RWKV_DIM, RWKV_DIM, DECAY_LORA, DECAY_LORA, ICLR_LORA, ICLR_LORA, GATE_LORA)
RWKV_COLS = 3 * RWKV_DIM + 2 * DECAY_LORA + 2 * ICLR_LORA + GATE_LORA
GA_COLS = (GA_Q_HEADS + 2 * GA_KV_HEADS) * HEAD_DIM
WA_COLS = (WA_Q_HEADS + 2 * WA_KV_HEADS) * HEAD_DIM
GATE_COLS = N_BRANCH * D_MODEL
IN_COLS = RWKV_COLS + GA_COLS + WA_COLS + GATE_COLS

kernel_name = 'hybrid_rwkv7_axial_gqa_swa_macaron'


def _split(t, sizes):
    return jnp.split(t, np.cumsum(sizes)[:-1].tolist(), axis=-1)


def rmsnorm(x, g):
    xf = x.astype(jnp.float32)
    y = xf * lax.rsqrt(jnp.mean(xf * xf, axis=-1, keepdims=True) + NORM_EPS)
    return (y * g.astype(jnp.float32)).astype(x.dtype)


def swiglu(h, w_up, w_down):
    gate, up = jnp.split(h @ w_up, 2, axis=-1)
    return (jax.nn.silu(gate) * up) @ w_down


def centred_shift(p):
    prev = jnp.pad(p[:, :-1], ((0, 0), (1, 0), (0, 0)))
    nxt = jnp.pad(p[:, 1:], ((0, 0), (0, 1), (0, 0)))
    return 0.5 * (prev + nxt)


def rope_angles(pos, dim):
    inv_freq = ROPE_THETA ** (-jnp.arange(0, dim, 2, dtype=jnp.float32) / dim)
    return pos.astype(jnp.float32)[:, None] * inv_freq[None, :]


def apply_rope(x, ang):
    ang = ang.reshape(ang.shape[0], *([1] * (x.ndim - 3)), ang.shape[-1])
    cos, sin = jnp.cos(ang), jnp.sin(ang)
    x1, x2 = jnp.split(x.astype(jnp.float32), 2, axis=-1)
    return jnp.concatenate([x1 * cos - x2 * sin, x2 * cos + x1 * sin], axis=-1).astype(x.dtype)


def apply_axial_rope(x, ang_row, ang_col):
    x_row, x_col = jnp.split(x, 2, axis=-1)
    return jnp.concatenate([apply_rope(x_row, ang_row), apply_rope(x_col, ang_col)], axis=-1)


def wkv_scan(r, w, k, v, a, b, reverse):
    bsz, _, nh, n = r.shape
    xs = tuple(jnp.moveaxis(t, 1, 0) for t in (r, w, k, v, a, b))

    def step(state, inp):
        r_t, w_t, k_t, v_t, a_t, b_t = inp
        sa = jnp.einsum('bhij,bhj->bhi', state, a_t)
        state = (state * w_t[:, :, None, :] + sa[..., None] * b_t[:, :, None, :]
                 + v_t[..., None] * k_t[:, :, None, :])
        return state, jnp.einsum('bhij,bhj->bhi', state, r_t)

    s0 = jnp.zeros((bsz, nh, n, n), jnp.float32)
    _, ys = lax.scan(step, s0, xs, reverse=reverse)
    return jnp.moveaxis(ys, 0, 1)


def rwkv_mixer(p, mu, w0, w2, a0, a2, g2, k_k, k_a, r_k, lnx_g, lnx_b):
    bsz, s, _ = p.shape
    p = p.astype(jnp.float32)
    p = p + mu * (centred_shift(p) - p)
    r, k, v, wl_f, wl_b, al_f, al_b, gl = _split(p, RWKV_SPLITS)
    heads = lambda t: t.reshape(bsz, s, RWKV_HEADS, HEAD_DIM)
    kk = heads(k * k_k)
    kk = kk / jnp.maximum(jnp.linalg.norm(kk, axis=-1, keepdims=True), 1e-12)
    kk = kk.reshape(bsz, s, RWKV_DIM)
    wkv = jnp.zeros((bsz, s, RWKV_HEADS, HEAD_DIM), jnp.float32)
    bonus = jnp.zeros((bsz, s, RWKV_HEADS, HEAD_DIM), jnp.float32)
    for d, (wl, al) in enumerate(((wl_f, al_f), (wl_b, al_b))):
        w_log = -jax.nn.softplus(-(w0[d] + jnp.tanh(wl) @ w2[d])) - 0.5
        decay = jnp.exp(-jnp.exp(w_log))
        a = jax.nn.sigmoid(a0[d] + al @ a2[d])
        k_d = k * (1.0 + (a - 1.0) * k_a)
        wkv = wkv + wkv_scan(heads(r), heads(decay), heads(k_d), heads(v),
                             heads(-kk), heads(kk * a), reverse=(d == 1))
        bonus = bonus + jnp.sum(heads(r) * heads(k_d) * r_k, axis=-1, keepdims=True) * heads(v)
    mean = jnp.mean(wkv, axis=-1, keepdims=True)
    var = jnp.mean(jnp.square(wkv - mean), axis=-1, keepdims=True)
    y = ((wkv - mean) * lax.rsqrt(var + LNX_EPS)).reshape(bsz, s, RWKV_DIM) * lnx_g + lnx_b
    y = y + bonus.reshape(bsz, s, RWKV_DIM)
    gate = jax.nn.sigmoid(gl) @ g2
    return y * gate


def global_attention(q, k, v):
    bsz, s, nkv, ng, hd = q.shape
    nblk = s // BLOCK
    scale = HEAD_DIM ** -0.5
    qb = jnp.moveaxis(q.reshape(bsz, nblk, BLOCK, nkv, ng, hd), 1, 0)

    def one_block(q_blk):
        sc = jnp.einsum('bqkgd,bskd->bkgqs', q_blk, k, preferred_element_type=jnp.float32) * scale
        pr = jax.nn.softmax(sc, axis=-1)
        return jnp.einsum('bkgqs,bskd->bqkgd', pr.astype(v.dtype), v)

    o = lax.map(one_block, qb)
    return jnp.moveaxis(o, 0, 1).reshape(bsz, s, nkv * ng * hd)


def window_attention(q, k, v, sink):
    bsz, s, nkv, ng, hd = q.shape
    nblk = s // BLOCK
    span = BLOCK + 2 * WINDOW
    scale = HEAD_DIM ** -0.5
    kp = jnp.pad(k, ((0, 0), (WINDOW, WINDOW), (0, 0), (0, 0)))
    vp = jnp.pad(v, ((0, 0), (WINDOW, WINDOW), (0, 0), (0, 0)))
    rel = jnp.arange(span)[None, :] - WINDOW - jnp.arange(BLOCK)[:, None]
    band = jnp.abs(rel) <= WINDOW
    sink_l = sink.astype(jnp.float32).reshape(nkv, ng)[None, :, :, None, None]

    def one_block(i):
        q_blk = lax.dynamic_slice_in_dim(q, i * BLOCK, BLOCK, axis=1)
        k_blk = lax.dynamic_slice_in_dim(kp, i * BLOCK, span, axis=1)
        v_blk = lax.dynamic_slice_in_dim(vp, i * BLOCK, span, axis=1)
        kpos = i * BLOCK - WINDOW + jnp.arange(span)
        valid = band & ((kpos >= 0) & (kpos < s))[None, :]
        sc = jnp.einsum('bqkgd,bjkd->bkgqj', q_blk, k_blk, preferred_element_type=jnp.float32) * scale
        sc = jnp.where(valid, sc, NEG_INF)
        m = jnp.maximum(jnp.max(sc, axis=-1, keepdims=True), sink_l)
        e = jnp.exp(sc - m)
        pr = e / (jnp.sum(e, axis=-1, keepdims=True) + jnp.exp(sink_l - m))
        return jnp.einsum('bkgqj,bjkd->bqkgd', pr.astype(v.dtype), v_blk)

    o = lax.map(one_block, jnp.arange(nblk))
    return jnp.moveaxis(o, 0, 1).reshape(bsz, s, nkv * ng * hd)


def setup_inputs(seed: int = 0) -> dict:
    key = jax.random.key(seed)
    ks = jax.random.split(key, 40)
    f32 = jnp.float32
    nrm = lambda kk, shape, scale: (jax.random.normal(kk, shape, f32) * scale).astype(f32)
    gain = lambda kk, shape: 1.0 + 0.02 * jax.random.normal(kk, shape, f32)
    L = DEPTH
    return {
        'x': jax.random.normal(ks[0], (BATCH, SEQ, D_MODEL), f32),
        'ffn1_norm': gain(ks[1], (L, D_MODEL)),
        'ffn1_w_up': nrm(ks[2], (L, D_MODEL, 2 * D_FF), D_MODEL ** -0.5),
        'ffn1_w_down': nrm(ks[3], (L, D_FF, D_MODEL), 0.5 * D_FF ** -0.5),
        'mix_norm': gain(ks[4], (L, D_MODEL)),
        'w_in': nrm(ks[5], (L, D_MODEL, IN_COLS), D_MODEL ** -0.5),
        'rwkv_mu': jax.random.uniform(ks[6], (L, RWKV_COLS), f32, 0.0, 1.0),
        'rwkv_w0': jax.random.uniform(ks[7], (L, 2, RWKV_DIM), f32, -6.0, -0.5),
        'rwkv_w2': nrm(ks[8], (L, 2, DECAY_LORA, RWKV_DIM), 0.1 * DECAY_LORA ** -0.5),
        'rwkv_a0': nrm(ks[9], (L, 2, RWKV_DIM), 0.1),
        'rwkv_a2': nrm(ks[10], (L, 2, ICLR_LORA, RWKV_DIM), 0.5 * ICLR_LORA ** -0.5),
        'rwkv_g2': nrm(ks[11], (L, GATE_LORA, RWKV_DIM), GATE_LORA ** -0.5),
        'rwkv_k_k': 0.85 + nrm(ks[12], (L, RWKV_DIM), 0.05),
        'rwkv_k_a': 1.0 + nrm(ks[13], (L, RWKV_DIM), 0.05),
        'rwkv_r_k': nrm(ks[14], (L, RWKV_HEADS, HEAD_DIM), 0.1),
        'rwkv_lnx_g': gain(ks[15], (L, RWKV_DIM)),
        'rwkv_lnx_b': nrm(ks[16], (L, RWKV_DIM), 0.02),
        'ga_q_norm': gain(ks[17], (L, HEAD_DIM)),
        'ga_k_norm': gain(ks[18], (L, HEAD_DIM)),
        'wa_sink': nrm(ks[19], (L, WA_Q_HEADS), 0.5),
        'w_o_rwkv': nrm(ks[20], (L, RWKV_DIM, D_MODEL), RWKV_DIM ** -0.5),
        'w_o_ga': nrm(ks[21], (L, GA_Q_HEADS * HEAD_DIM, D_MODEL), (GA_Q_HEADS * HEAD_DIM) ** -0.5),
        'w_o_wa': nrm(ks[22], (L, WA_Q_HEADS * HEAD_DIM, D_MODEL), (WA_Q_HEADS * HEAD_DIM) ** -0.5),
        'w_out': nrm(ks[23], (L, D_MODEL, D_MODEL), 0.5 * D_MODEL ** -0.5),
        'ffn2_norm': gain(ks[24], (L, D_MODEL)),
        'ffn2_w_up': nrm(ks[25], (L, D_MODEL, 2 * D_FF), D_MODEL ** -0.5),
        'ffn2_w_down': nrm(ks[26], (L, D_FF, D_MODEL), 0.5 * D_FF ** -0.5),
        'final_norm': gain(ks[27], (D_MODEL,)),
    }


def reference(x, ffn1_norm, ffn1_w_up, ffn1_w_down, mix_norm, w_in, rwkv_mu, rwkv_w0, rwkv_w2,
              rwkv_a0, rwkv_a2, rwkv_g2, rwkv_k_k, rwkv_k_a, rwkv_r_k, rwkv_lnx_g, rwkv_lnx_b,
              ga_q_norm, ga_k_norm, wa_sink, w_o_rwkv, w_o_ga, w_o_wa, w_out,
              ffn2_norm, ffn2_w_up, ffn2_w_down, final_norm):
    bsz, s, dm = x.shape
    rows = s // GRID_W
    t = jnp.arange(s)
    row = jnp.repeat(jnp.arange(rows), GRID_W)
    col = jnp.tile(jnp.arange(GRID_W), rows)
    ang_row = rope_angles(row, HEAD_DIM // 2)
    ang_col = rope_angles(col, HEAD_DIM // 2)
    ang_1d = rope_angles(t, HEAD_DIM)
    ga_g = GA_Q_HEADS // GA_KV_HEADS
    wa_g = WA_Q_HEADS // WA_KV_HEADS

    for l in range(DEPTH):
        x = x + 0.5 * swiglu(rmsnorm(x, ffn1_norm[l]), ffn1_w_up[l], ffn1_w_down[l])

        h = rmsnorm(x, mix_norm[l])
        p = h @ w_in[l]
        p_rwkv, p_ga, p_wa, p_gate = _split(p, (RWKV_COLS, GA_COLS, WA_COLS, GATE_COLS))

        y_a = rwkv_mixer(p_rwkv, rwkv_mu[l], rwkv_w0[l], rwkv_w2[l], rwkv_a0[l], rwkv_a2[l],
                         rwkv_g2[l], rwkv_k_k[l], rwkv_k_a[l], rwkv_r_k[l],
                         rwkv_lnx_g[l], rwkv_lnx_b[l]).astype(x.dtype)

        qb, kb, vb = _split(p_ga, (GA_Q_HEADS * HEAD_DIM, GA_KV_HEADS * HEAD_DIM, GA_KV_HEADS * HEAD_DIM))
        qb = qb.reshape(bsz, s, GA_KV_HEADS, ga_g, HEAD_DIM)
        kb = kb.reshape(bsz, s, GA_KV_HEADS, HEAD_DIM)
        vb = vb.reshape(bsz, s, GA_KV_HEADS, HEAD_DIM)
        qb = apply_axial_rope(rmsnorm(qb, ga_q_norm[l]), ang_row, ang_col)
        kb = apply_axial_rope(rmsnorm(kb, ga_k_norm[l]), ang_row, ang_col)
        y_b = global_attention(qb, kb, vb)

        qc, kc, vc = _split(p_wa, (WA_Q_HEADS * HEAD_DIM, WA_KV_HEADS * HEAD_DIM, WA_KV_HEADS * HEAD_DIM))
        qc = apply_rope(qc.reshape(bsz, s, WA_KV_HEADS, wa_g, HEAD_DIM), ang_1d)
        kc = apply_rope(kc.reshape(bsz, s, WA_KV_HEADS, HEAD_DIM), ang_1d)
        vc = vc.reshape(bsz, s, WA_KV_HEADS, HEAD_DIM)
        y_c = window_attention(qc, kc, vc, wa_sink[l])

        gates = jax.nn.sigmoid(p_gate).reshape(bsz, s, N_BRANCH, dm)
        merged = (gates[:, :, 0] * (y_a @ w_o_rwkv[l]) + gates[:, :, 1] * (y_b @ w_o_ga[l])
                  + gates[:, :, 2] * (y_c @ w_o_wa[l]))
        x = x + merged @ w_out[l]

        x = x + 0.5 * swiglu(rmsnorm(x, ffn2_norm[l]), ffn2_w_up[l], ffn2_w_down[l])

    return rmsnorm(x, final_norm)
```

```python
import functools

import jax
import jax.numpy as jnp
import numpy as np
from jax import lax
from jax.experimental import pallas as pl
from jax.experimental.pallas import tpu as pltpu

F32 = jnp.float32
BF16 = jnp.bfloat16

D_MODEL = 1024
HEAD_DIM = 64
RWKV_DIM = 512
RWKV_COLS = 1920
GA_COLS = 768
WA_COLS = 768
MIX_COLS = RWKV_COLS + GA_COLS + WA_COLS
D_FF = 2816
GRID_W = 64
WINDOW = 128
ROPE_THETA = 10000.0
NORM_EPS = 1e-6
LNX_EPS = HEAD_DIM * 1e-5
NEG_INF = -1e30
ATTN_SCALE = HEAD_DIM ** -0.5

CHUNK = 64
QUAD = 4 * HEAD_DIM
LANES = 128
VMEM_LIMIT = 56 * 1024 * 1024

TOKEN_TILE = 512
FF_CHUNK = 1408
PREP_TILE = 128
SCAN_CHUNKS = 4
GA_Q_TILE = 128
WA_Q_TILE = 256


def _params(*sem):
    return pltpu.CompilerParams(dimension_semantics=sem, vmem_limit_bytes=VMEM_LIMIT)


def _resident(shape):
    nd = len(shape)
    return pl.BlockSpec(shape, lambda *_: (0,) * nd, pipeline_mode=pl.Buffered(1))


def _rmsnorm(x, g):
    return x * lax.rsqrt(jnp.mean(x * x, axis=-1, keepdims=True) + NORM_EPS) * g


def _dot(a, b):
    return jnp.dot(a, b, preferred_element_type=F32)


def _dot_nt(a, b):
    return lax.dot_general(a, b, (((1,), (1,)), ((), ())), preferred_element_type=F32)


def _dot_tn(a, b):
    return lax.dot_general(a, b, (((0,), (0,)), ((), ())), preferred_element_type=F32)


def _split_bf16(x, terms):
    out = []
    for _ in range(terms - 1):
        hi = x.astype(BF16)
        out.append(hi)
        x = x - hi.astype(F32)
    out.append(x.astype(BF16))
    return out


def _head_sum(x, ones_bd, terms):
    w = x.shape[-1]
    ones = ones_bd[:w, :w]
    acc = None
    for t in _split_bf16(x, terms):
        y = _dot(t, ones)
        acc = y if acc is None else acc + y
    return acc


def _ffn_body(x_ref, g_ref, wup_ref, wdn_ref, gf_ref, o_ref, act_ref, *, final_norm):
    x = x_ref[...]
    h = _rmsnorm(x, g_ref[...]).astype(BF16)
    for c in range(D_FF // FF_CHUNK):
        lo, hi = c * FF_CHUNK, (c + 1) * FF_CHUNK
        gate = _dot(h, wup_ref[:, lo:hi])
        up = _dot(h, wup_ref[:, D_FF + lo:D_FF + hi])
        act_ref[:, lo:hi] = (gate * jax.nn.sigmoid(gate) * up).astype(BF16)
    y = x + 0.5 * _dot(act_ref[...], wdn_ref[...])
    if final_norm:
        y = _rmsnorm(y, gf_ref[...])
    o_ref[...] = y


def _ffn(x, g, w_up, w_down, g_final, final_norm):
    t, d = x.shape
    tm = min(TOKEN_TILE, t)
    row = pl.BlockSpec((tm, d), lambda i: (i, 0))
    return pl.pallas_call(
        functools.partial(_ffn_body, final_norm=final_norm),
        grid=(t // tm,),
        in_specs=[row, _resident((1, d)), _resident(w_up.shape), _resident(w_down.shape), _resident((1, d))],
        out_specs=row,
        out_shape=jax.ShapeDtypeStruct((t, d), F32),
        scratch_shapes=[pltpu.VMEM((tm, D_FF), BF16)],
        compiler_params=_params("parallel"),
        name="ffn_final" if final_norm else "ffn",
    )(x, g, w_up, w_down, g_final)


def _rope(t, cos, sin_signed, half):
    w = t.shape[-1]
    lane = lax.broadcasted_iota(jnp.int32, t.shape, 1)
    first = (lane % (2 * half)) < half
    partner = jnp.where(first, pltpu.roll(t, w - half, 1), pltpu.roll(t, half, 1))
    return t * cos + partner * sin_signed


def _dup_heads(t):
    lane = lax.broadcasted_iota(jnp.int32, t.shape, 1)
    swapped = pltpu.roll(t, HEAD_DIM, 1)
    low = lane < HEAD_DIM
    return jnp.where(low, t, swapped), jnp.where(low, swapped, t)


def _inproj_body(x_ref, g_ref, w_ref, ones_ref, gq_ref, gk_ref, cga_ref, sga_ref, cwa_ref, swa_ref,
                 prw_ref, gaq_ref, gakv_ref, waq_ref, wakv_ref):
    x = x_ref[...]
    h = _rmsnorm(x, g_ref[...]).astype(BF16)
    prw_ref[...] = _dot(h, w_ref[:, :RWKV_COLS])
    ga = _dot(h, w_ref[:, RWKV_COLS:RWKV_COLS + GA_COLS])
    wa = _dot(h, w_ref[:, RWKV_COLS + GA_COLS:MIX_COLS])
    ones = ones_ref[...]

    def head_norm(t, gain):
        ms = _head_sum(t * t, ones, 1) * (1.0 / HEAD_DIM)
        return t * lax.rsqrt(ms + NORM_EPS) * gain

    tile4 = lambda tab: jnp.concatenate([tab] * 4, axis=1)
    cga, sga = cga_ref[...], sga_ref[...]
    q = _rope(head_norm(ga[:, :512], gq_ref[...]), tile4(cga), tile4(sga), 16)
    k = _rope(head_norm(ga[:, 512:640], gk_ref[:, :LANES]), cga, sga, 16)
    gaq_ref[...] = (q * ATTN_SCALE).astype(BF16)
    k0, k1 = _dup_heads(k)
    v0, v1 = _dup_heads(ga[:, 640:768])
    gakv_ref[...] = jnp.concatenate([k0, k1, v0, v1], axis=1).astype(BF16)
    cwa, swa = cwa_ref[...], swa_ref[...]
    q = _rope(wa[:, :512], tile4(cwa), tile4(swa), 32)
    k = _rope(wa[:, 512:640], cwa, swa, 32)
    waq_ref[...] = (q * ATTN_SCALE).astype(BF16)
    k0, k1 = _dup_heads(k)
    v0, v1 = _dup_heads(wa[:, 640:768])
    wakv_ref[...] = jnp.concatenate([k0, k1, v0, v1], axis=1).astype(BF16)


def _inproj(x, g, w, ones_bd, gq, gk, tabs, seq):
    t, d = x.shape
    tm = min(TOKEN_TILE, seq)
    per_seq = seq // tm
    row = lambda w_: pl.BlockSpec((tm, w_), lambda i: (i, 0))
    tab = pl.BlockSpec((tm, LANES), lambda i: (i % per_seq, 0))
    return pl.pallas_call(
        _inproj_body,
        grid=(t // tm,),
        in_specs=[row(d), _resident((1, d)), _resident(w.shape), _resident(ones_bd.shape),
                  _resident(gq.shape), _resident(gk.shape), tab, tab, tab, tab],
        out_specs=[row(RWKV_COLS), row(512), row(512), row(512), row(512)],
        out_shape=[jax.ShapeDtypeStruct((t, RWKV_COLS), F32)] + [jax.ShapeDtypeStruct((t, 512), BF16)] * 4,
        compiler_params=_params("parallel"),
        name="inproj",
    )(x, g, w, ones_bd, gq, gk, *tabs)


def _block_diag4(x):
    head = lax.broadcasted_iota(jnp.int32, x.shape, 1) // HEAD_DIM
    return jnp.concatenate([jnp.where(head == h, x, jnp.zeros_like(x)) for h in range(4)], axis=0)


def _bd16(x):
    return _block_diag4(x).astype(BF16)


def _rwkv_prep_body(p_ref, hp_ref, hn_ref, mu_ref, w0_ref, w2_ref, a0_ref, a2_ref, g2_ref, kk_ref, ka_ref, rk_ref,
                    ones_ref, *out_refs, ts):
    dir_refs = (out_refs[0:8], out_refs[8:16])
    v_ref, bonus_ref, gate_ref = out_refs[16:19]
    i = pl.program_id(1)
    n = pl.num_programs(1)
    p = p_ref[0]
    prev_row = jnp.where(i > 0, hp_ref[0, 7:8, :], 0.0)
    next_row = jnp.where(i < n - 1, hn_ref[0, 0:1, :], 0.0)
    row = lax.broadcasted_iota(jnp.int32, p.shape, 0)
    prev = jnp.where(row == 0, prev_row, pltpu.roll(p, 1, 0))
    nxt = jnp.where(row == ts - 1, next_row, pltpu.roll(p, ts - 1, 0))
    pm = p + mu_ref[...] * (0.5 * (prev + nxt) - p)

    r, k, v = pm[:, 0:512], pm[:, 512:1024], pm[:, 1024:1536]
    wl, al, gl = pm[:, 1536:1664], pm[:, 1664:1792], pm[:, 1792:1920]
    ones = ones_ref[...]

    u = w0_ref[...] + _dot(jnp.tanh(wl).astype(BF16), w2_ref[...])
    logw = (-float(np.exp(-0.5))) * jax.nn.sigmoid(u)
    asig = jax.nn.sigmoid(a0_ref[...] + _dot(al.astype(BF16), a2_ref[...]))
    kkr = k * kk_ref[...]
    kk = kkr * lax.rsqrt(jnp.maximum(_head_sum(kkr * kkr, ones, 1), 1e-24))
    k_a = ka_ref[...]
    kd = [k * (1.0 + (asig[:, d * 512:(d + 1) * 512] - 1.0) * k_a) for d in range(2)]
    bonus_ref[0] = _head_sum(r * rk_ref[...] * (kd[0] + kd[1]), ones, 2) * v
    gate_ref[0] = _dot(jax.nn.sigmoid(gl).astype(BF16), g2_ref[...])
    v_ref[0] = v.astype(BF16)

    rr = lax.broadcasted_iota(jnp.int32, (ts, ts), 0)
    cc = lax.broadcasted_iota(jnp.int32, (ts, ts), 1)
    same = (rr // CHUNK) == (cc // CHUNK)
    tt = lax.broadcasted_iota(jnp.int32, (CHUNK, QUAD), 0)
    ss = lax.broadcasted_iota(jnp.int32, (CHUNK, QUAD), 1) % CHUNK
    eye = jnp.where(tt == ss, 1.0, 0.0)

    for d in range(2):
        at_ref, rt_ref, mrb_ref, uv_ref, yv_ref, bp_ref, kp_ref, pt_ref = dir_refs[d]
        before = (cc <= rr) if d == 0 else (cc >= rr)
        lhs = jnp.concatenate([jnp.where(same & before, 1.0, 0.0), jnp.where(same, 1.0, 0.0)], axis=0).astype(BF16)
        lw = logw[:, d * 512:(d + 1) * 512]
        sums = None
        for term in _split_bf16(lw, 3):
            y = _dot(lhs, term)
            sums = y if sums is None else sums + y
        cl, tot = sums[:ts], sums[ts:]
        e_in = jnp.exp(cl)
        e_neg = jnp.exp(-cl)
        asig_d = asig[:, d * 512:(d + 1) * 512]
        a_t = -kk * jnp.exp(cl - lw)
        b_raw = kk * asig_d
        b_t = b_raw * e_neg
        k_t = kd[d] * e_neg
        r_t = r * e_in
        e_out = jnp.exp(tot - cl)
        bp_ref[0] = (b_raw * e_out).astype(BF16)
        kp_ref[0] = (kd[d] * e_out).astype(BF16)
        rt_ref[0] = r_t.astype(BF16)
        strict = (ss < tt) if d == 0 else (ss > tt)
        incl = (ss <= tt) if d == 0 else (ss >= tt)
        for c in range(ts // CHUNK):
            rs = slice(c * CHUNK, (c + 1) * CHUNK)
            last = (c + 1) * CHUNK - 1
            pt_ref[0, c] = jnp.exp(tot[last:last + 1, :])
            for qd in range(RWKV_DIM // QUAD):
                ls = slice(qd * QUAD, (qd + 1) * QUAD)
                at = a_t[rs, ls]
                x2 = jnp.concatenate([at, r_t[rs, ls]], axis=0).astype(BF16)
                w_b = _dot_nt(x2, _bd16(b_t[rs, ls]))
                w_k = _dot_nt(x2, _bd16(k_t[rs, ls]))
                a_ab = jnp.where(strict, w_b[:CHUNK], 0.0)
                mrb = jnp.where(incl, w_b[CHUNK:], 0.0)
                lk = jnp.concatenate([jnp.where(strict, w_k[:CHUNK], 0.0), jnp.where(incl, w_k[CHUNK:], 0.0)], axis=0)
                gy = _dot(lk.astype(BF16), _bd16(v[rs, ls]))
                g_in, yv = gy[:CHUNK], gy[CHUNK:]
                tm = eye + a_ab
                pw = _dot(a_ab.astype(BF16), _bd16(a_ab))
                for _ in range(4):
                    both = _dot(jnp.concatenate([pw, tm], axis=0).astype(BF16), _bd16(pw))
                    tm = tm + both[CHUNK:]
                    pw = both[:CHUNK]
                tm = tm + _dot(tm.astype(BF16), _bd16(pw))
                tm16 = tm.astype(BF16)
                uv_ref[0, rs, ls] = _dot(tm16, _bd16(g_in))
                at_ref[0, rs, ls] = _dot(tm16, _bd16(at)).astype(BF16)
                mrb_ref[0, rs, ls] = mrb.astype(BF16)
                yv_ref[0, rs, ls] = yv


def _rwkv_prep(p, mu, w0, w2bd, a0, a2bd, g2, k_k, k_a, r_k, ones_bd):
    b, s, _ = p.shape
    ts = min(PREP_TILE, s)
    nt = s // ts
    h8 = ts // 8
    main = pl.BlockSpec((1, ts, RWKV_COLS), lambda bi, i: (bi, i, 0))
    halo_prev = pl.BlockSpec((1, 8, RWKV_COLS), lambda bi, i: (bi, jnp.maximum(i * h8 - 1, 0), 0))
    halo_next = pl.BlockSpec((1, 8, RWKV_COLS), lambda bi, i: (bi, jnp.minimum((i + 1) * h8, s // 8 - 1), 0))
    tok = pl.BlockSpec((1, ts, RWKV_DIM), lambda bi, i: (bi, i, 0))
    ptot = pl.BlockSpec((1, ts // CHUNK, 1, RWKV_DIM), lambda bi, i: (bi, i, 0, 0))
    tok16 = jax.ShapeDtypeStruct((b, s, RWKV_DIM), BF16)
    tok32 = jax.ShapeDtypeStruct((b, s, RWKV_DIM), F32)
    pt32 = jax.ShapeDtypeStruct((b, s // CHUNK, 1, RWKV_DIM), F32)
    per_dir_specs = [tok, tok, tok, tok, tok, tok, tok, ptot]
    per_dir_shapes = [tok16, tok16, tok16, tok32, tok32, tok16, tok16, pt32]
    consts = [mu, w0, w2bd, a0, a2bd, g2, k_k, k_a, r_k, ones_bd]
    return pl.pallas_call(
        functools.partial(_rwkv_prep_body, ts=ts),
        grid=(b, nt),
        in_specs=[main, halo_prev, halo_next] + [_resident(c.shape) for c in consts],
        out_specs=per_dir_specs * 2 + [tok, tok, tok],
        out_shape=per_dir_shapes * 2 + [tok16, tok32, tok32],
        compiler_params=_params("parallel", "parallel"),
        name="rwkv_prep",
    )(p, p, p, *consts)


def _rwkv_scan_body(*refs, nc):
    fwd, bwd = refs[0:8], refs[8:16]
    vf_ref, vb_ref = refs[16:18]
    yf_ref, yb_ref = refs[18:20]
    st_ref = refs[20]
    step = pl.program_id(1)

    @pl.when(step == 0)
    def _():
        st_ref[...] = jnp.zeros_like(st_ref)

    rr = lax.broadcasted_iota(jnp.int32, (QUAD, QUAD), 0) // HEAD_DIM
    cc = lax.broadcasted_iota(jnp.int32, (QUAD, QUAD), 1) // HEAD_DIM
    diag = rr == cc
    for d, (in_refs, v_ref, y_ref) in enumerate(((fwd, vf_ref, yf_ref), (bwd, vb_ref, yb_ref))):
        at_ref, rt_ref, mrb_ref, uv_ref, yv_ref, bp_ref, kp_ref, pt_ref = in_refs
        order = range(nc) if d == 0 else range(nc - 1, -1, -1)
        for qd in range(RWKV_DIM // QUAD):
            ls = slice(qd * QUAD, (qd + 1) * QUAD)
            st = st_ref[d, qd]
            for c in order:
                rs = slice(c * CHUNK, (c + 1) * CHUNK)
                x2 = jnp.concatenate([at_ref[0, rs, ls], rt_ref[0, rs, ls]], axis=0)
                z = _dot_nt(x2, st.astype(BF16))
                u = z[:CHUNK] + uv_ref[0, rs, ls]
                y_ref[0, rs, ls] = z[CHUNK:] + _dot(mrb_ref[0, rs, ls], _bd16(u)) + yv_ref[0, rs, ls]
                uv2 = jnp.concatenate([u.astype(BF16), v_ref[0, rs, ls]], axis=0)
                bk2 = jnp.concatenate([bp_ref[0, rs, ls], kp_ref[0, rs, ls]], axis=0)
                st = st * pt_ref[0, c][:, ls] + jnp.where(diag, _dot_tn(uv2, bk2), 0.0)
            st_ref[d, qd] = st


def _rwkv_scan(fwd_arrs, bwd_arrs, v16):
    b, s, _ = v16.shape
    nc = min(SCAN_CHUNKS, s // CHUNK)
    rows = nc * CHUNK
    steps = s // rows
    f_tok = pl.BlockSpec((1, rows, RWKV_DIM), lambda bi, i: (bi, i, 0))
    b_tok = pl.BlockSpec((1, rows, RWKV_DIM), lambda bi, i: (bi, steps - 1 - i, 0))
    f_pt = pl.BlockSpec((1, nc, 1, RWKV_DIM), lambda bi, i: (bi, i, 0, 0))
    b_pt = pl.BlockSpec((1, nc, 1, RWKV_DIM), lambda bi, i: (bi, steps - 1 - i, 0, 0))
    y_shape = jax.ShapeDtypeStruct((b, s, RWKV_DIM), F32)
    return pl.pallas_call(
        functools.partial(_rwkv_scan_body, nc=nc),
        grid=(b, steps),
        in_specs=[f_tok] * 7 + [f_pt] + [b_tok] * 7 + [b_pt] + [f_tok, b_tok],
        out_specs=[f_tok, b_tok],
        out_shape=[y_shape, y_shape],
        scratch_shapes=[pltpu.VMEM((2, RWKV_DIM // QUAD, QUAD, QUAD), F32)],
        compiler_params=_params("parallel", "arbitrary"),
        name="rwkv_scan",
    )(*fwd_arrs, *bwd_arrs, v16, v16)


def _rwkv_post_body(yf_ref, yb_ref, bonus_ref, gate_ref, lg_ref, lb_ref, ones_ref, o_ref):
    ones = ones_ref[...]
    wkv = yf_ref[...] + yb_ref[...]
    mean = _head_sum(wkv, ones, 3) * (1.0 / HEAD_DIM)
    dev = wkv - mean
    var = _head_sum(dev * dev, ones, 2) * (1.0 / HEAD_DIM)
    y = dev * lax.rsqrt(var + LNX_EPS) * lg_ref[...] + lb_ref[...] + bonus_ref[...]
    o_ref[...] = (y * gate_ref[...]).astype(BF16)


def _rwkv_post(yf, yb, bonus, gate, lnx_g, lnx_b, ones_bd):
    t, w = yf.shape
    tm = min(TOKEN_TILE, t)
    row = pl.BlockSpec((tm, w), lambda i: (i, 0))
    return pl.pallas_call(
        _rwkv_post_body,
        grid=(t // tm,),
        in_specs=[row, row, row, row, _resident((1, w)), _resident((1, w)), _resident(ones_bd.shape)],
        out_specs=row,
        out_shape=jax.ShapeDtypeStruct((t, w), BF16),
        compiler_params=_params("parallel"),
        name="rwkv_post",
    )(yf, yb, bonus, gate, lnx_g, lnx_b, ones_bd)


def _group_queries(q_ref, g, tq):
    lane = lax.broadcasted_iota(jnp.int32, (tq, LANES), 1)
    parts = []
    for pair in range(2):
        qp = q_ref[0, :, (2 * g + pair) * LANES:(2 * g + pair + 1) * LANES]
        parts.append(jnp.where(lane < HEAD_DIM, qp, jnp.zeros_like(qp)))
        parts.append(jnp.where(lane >= HEAD_DIM, qp, jnp.zeros_like(qp)))
    return jnp.concatenate(parts, axis=0)


def _ungroup(o, tq):
    lane = lax.broadcasted_iota(jnp.int32, (tq, LANES), 1)
    low = lane < HEAD_DIM
    return jnp.concatenate([jnp.where(low, o[0:tq], o[tq:2 * tq]),
                            jnp.where(low, o[2 * tq:3 * tq], o[3 * tq:4 * tq])], axis=1)


def _ga_body(q_ref, kv_ref, o_ref, *, tq):
    for g in range(2):
        kd = kv_ref[0, :, g * LANES:(g + 1) * LANES]
        vd = kv_ref[0, :, (2 + g) * LANES:(3 + g) * LANES]
        sc = _dot_nt(_group_queries(q_ref, g, tq), kd)
        e = jnp.exp(sc - jnp.max(sc, axis=-1, keepdims=True))
        o = _dot(e.astype(BF16), vd) / jnp.sum(e, axis=-1, keepdims=True)
        o_ref[0, :, g * 2 * LANES:(g + 1) * 2 * LANES] = _ungroup(o, tq).astype(BF16)


def _global_attention(q, kv):
    b, s, w = q.shape
    tq = min(GA_Q_TILE, s)
    qs = pl.BlockSpec((1, tq, w), lambda bi, i: (bi, i, 0))
    kvs = pl.BlockSpec((1, s, w), lambda bi, i: (bi, 0, 0))
    return pl.pallas_call(
        functools.partial(_ga_body, tq=tq),
        grid=(b, s // tq),
        in_specs=[qs, kvs],
        out_specs=qs,
        out_shape=jax.ShapeDtypeStruct((b, s, w), BF16),
        compiler_params=_params("parallel", "arbitrary"),
        name="global_attn",
    )(q, kv)


def _wa_body(sink_ref, q_ref, kv_ref, o_ref, *, tq, span, seq):
    i = pl.program_id(1)
    start = pl.multiple_of(jnp.clip(i * tq - WINDOW, 0, seq - span), WINDOW)
    rows = lax.broadcasted_iota(jnp.int32, (4 * tq, span), 0)
    cols = lax.broadcasted_iota(jnp.int32, (4 * tq, span), 1)
    rel = (start + cols) - (i * tq + rows % tq)
    valid = jnp.abs(rel) <= WINDOW
    for g in range(2):
        kd = kv_ref[0, pl.ds(start, span), g * LANES:(g + 1) * LANES]
        vd = kv_ref[0, pl.ds(start, span), (2 + g) * LANES:(3 + g) * LANES]
        sc = jnp.where(valid, _dot_nt(_group_queries(q_ref, g, tq), kd), NEG_INF)
        sink = jnp.concatenate([jnp.full((tq, 1), sink_ref[4 * g + h], F32) for h in range(4)], axis=0)
        m = jnp.maximum(jnp.max(sc, axis=-1, keepdims=True), sink)
        e = jnp.exp(sc - m)
        denom = jnp.sum(e, axis=-1, keepdims=True) + jnp.exp(sink - m)
        o = _dot(e.astype(BF16), vd) / denom
        o_ref[0, :, g * 2 * LANES:(g + 1) * 2 * LANES] = _ungroup(o, tq).astype(BF16)


def _window_attention(q, kv, sink):
    b, s, w = q.shape
    tq = min(WA_Q_TILE, s)
    span = min(tq + 2 * WINDOW, s)
    qs = pl.BlockSpec((1, tq, w), lambda bi, i, *_: (bi, i, 0))
    kvs = pl.BlockSpec((1, s, w), lambda bi, i, *_: (bi, 0, 0))
    return pl.pallas_call(
        functools.partial(_wa_body, tq=tq, span=span, seq=s),
        grid_spec=pltpu.PrefetchScalarGridSpec(
            num_scalar_prefetch=1, grid=(b, s // tq), in_specs=[qs, kvs], out_specs=qs),
        out_shape=jax.ShapeDtypeStruct((b, s, w), BF16),
        compiler_params=_params("parallel", "arbitrary"),
        name="window_attn",
    )(sink, q, kv)


def _merge_body(x_ref, g_ref, wg_ref, ya_ref, yb_ref, yc_ref, woa_ref, wob_ref, woc_ref, wout_ref, o_ref):
    x = x_ref[...]
    h = _rmsnorm(x, g_ref[...]).astype(BF16)
    merged = None
    for j, (y_ref, wo_ref) in enumerate(((ya_ref, woa_ref), (yb_ref, wob_ref), (yc_ref, woc_ref))):
        gate = jax.nn.sigmoid(_dot(h, wg_ref[:, j * D_MODEL:(j + 1) * D_MODEL]))
        term = gate * _dot(y_ref[...], wo_ref[...])
        merged = term if merged is None else merged + term
    o_ref[...] = x + _dot(merged.astype(BF16), wout_ref[...])


def _merge(x, g, wg, ya, yb, yc, woa, wob, woc, wout):
    t, d = x.shape
    tm = min(TOKEN_TILE, t)
    row = lambda w_: pl.BlockSpec((tm, w_), lambda i: (i, 0))
    return pl.pallas_call(
        _merge_body,
        grid=(t // tm,),
        in_specs=[row(d), _resident((1, d)), _resident(wg.shape), row(512), row(512), row(512),
                  _resident(woa.shape), _resident(wob.shape), _resident(woc.shape), _resident(wout.shape)],
        out_specs=row(d),
        out_shape=jax.ShapeDtypeStruct((t, d), F32),
        compiler_params=_params("parallel"),
        name="merge",
    )(x, g, wg, ya, yb, yc, woa, wob, woc, wout)


def _rope_tables(seq):
    t = jnp.arange(seq)

    def angles(pos, dim):
        inv_freq = ROPE_THETA ** (-jnp.arange(0, dim, 2, dtype=F32) / dim)
        return pos.astype(F32)[:, None] * inv_freq[None, :]

    ang_row, ang_col = angles(t // GRID_W, HEAD_DIM // 2), angles(t % GRID_W, HEAD_DIM // 2)
    ang_1d = angles(t, HEAD_DIM)
    ga_cos = jnp.concatenate([jnp.cos(ang_row)] * 2 + [jnp.cos(ang_col)] * 2, axis=1)
    ga_sin = jnp.concatenate([-jnp.sin(ang_row), jnp.sin(ang_row), -jnp.sin(ang_col), jnp.sin(ang_col)], axis=1)
    wa_cos = jnp.concatenate([jnp.cos(ang_1d)] * 2, axis=1)
    wa_sin = jnp.concatenate([-jnp.sin(ang_1d), jnp.sin(ang_1d)], axis=1)
    return tuple(jnp.concatenate([a, a], axis=1) for a in (ga_cos, ga_sin, wa_cos, wa_sin))


def _two_way(w):
    z = jnp.zeros_like(w[0])
    return jnp.concatenate([jnp.concatenate([w[0], z], axis=1), jnp.concatenate([z, w[1]], axis=1)], axis=0)


def kernel(x, ffn1_norm, ffn1_w_up, ffn1_w_down, mix_norm, w_in, rwkv_mu, rwkv_w0, rwkv_w2, rwkv_a0, rwkv_a2, rwkv_g2, rwkv_k_k, rwkv_k_a, rwkv_r_k, rwkv_lnx_g, rwkv_lnx_b, ga_q_norm, ga_k_norm, wa_sink, w_o_rwkv, w_o_ga, w_o_wa, w_out, ffn2_norm, ffn2_w_up, ffn2_w_down, final_norm):
    bsz, seq, dm = x.shape
    depth = w_in.shape[0]
    tokens = bsz * seq
    tabs = _rope_tables(seq)
    idx = jnp.arange(512)
    ones_bd = (idx[:, None] // HEAD_DIM == idx[None, :] // HEAD_DIM).astype(BF16)
    row1 = lambda a: a.reshape(1, -1)
    final_g = row1(final_norm)

    xt = x.reshape(tokens, dm)
    for l in range(depth):
        xt = _ffn(xt, row1(ffn1_norm[l]), ffn1_w_up[l].astype(BF16), ffn1_w_down[l].astype(BF16), final_g, False)

        w_mix = w_in[l, :, :MIX_COLS].astype(BF16)
        w_gate = w_in[l, :, MIX_COLS:].astype(BF16)
        p_rwkv, ga_q, ga_kv, wa_q, wa_kv = _inproj(
            xt, row1(mix_norm[l]), w_mix, ones_bd, row1(jnp.tile(ga_q_norm[l], 8)), row1(jnp.tile(ga_k_norm[l], 8)),
            tabs, seq)

        outs = _rwkv_prep(
            p_rwkv.reshape(bsz, seq, RWKV_COLS), row1(rwkv_mu[l]), row1(rwkv_w0[l]), _two_way(rwkv_w2[l]).astype(BF16),
            row1(rwkv_a0[l]), _two_way(rwkv_a2[l]).astype(BF16), rwkv_g2[l].astype(BF16), row1(rwkv_k_k[l]),
            row1(rwkv_k_a[l]), row1(rwkv_r_k[l]), ones_bd)
        v16, bonus, gate = outs[16:19]
        y_f, y_b = _rwkv_scan(outs[0:8], outs[8:16], v16)
        flat = lambda a: a.reshape(tokens, RWKV_DIM)
        y_a = _rwkv_post(flat(y_f), flat(y_b), flat(bonus), flat(gate), row1(rwkv_lnx_g[l]), row1(rwkv_lnx_b[l]), ones_bd)

        y_g = _global_attention(ga_q.reshape(bsz, seq, 512), ga_kv.reshape(bsz, seq, 512))
        y_w = _window_attention(wa_q.reshape(bsz, seq, 512), wa_kv.reshape(bsz, seq, 512), wa_sink[l])

        xt = _merge(xt, row1(mix_norm[l]), w_gate, y_a, flat(y_g), flat(y_w), w_o_rwkv[l].astype(BF16),
                    w_o_ga[l].astype(BF16), w_o_wa[l].astype(BF16), w_out[l].astype(BF16))

        xt = _ffn(xt, row1(ffn2_norm[l]), ffn2_w_up[l].astype(BF16), ffn2_w_down[l].astype(BF16), final_g,
                  l == depth - 1)
    return xt.reshape(bsz, seq, dm)
```

```python
import functools

import jax
import jax.numpy as jnp
import numpy as np
from jax import lax
from jax.experimental import pallas as pl
from jax.experimental.pallas import tpu as pltpu

F32 = jnp.float32
BF16 = jnp.bfloat16

D_MODEL = 1024
HEAD_DIM = 64
RWKV_DIM = 512
RWKV_COLS = 1920
GA_COLS = 768
WA_COLS = 768
MIX_COLS = RWKV_COLS + GA_COLS + WA_COLS
D_FF = 2816
GRID_W = 64
WINDOW = 128
ROPE_THETA = 10000.0
NORM_EPS = 1e-6
LNX_EPS = HEAD_DIM * 1e-5
NEG_INF = -1e30
ATTN_SCALE = HEAD_DIM ** -0.5

CHUNK = 64
QUAD = 4 * HEAD_DIM
LANES = 128
VMEM_LIMIT = 56 * 1024 * 1024

TOKEN_TILE = 512
FF_CHUNK = 1408
PREP_TILE = 128
SCAN_CHUNKS = 4
GA_Q_TILE = 128
WA_Q_TILE = 256


def _params(*sem):
    return pltpu.CompilerParams(dimension_semantics=sem, vmem_limit_bytes=VMEM_LIMIT)


def _resident(shape):
    nd = len(shape)
    return pl.BlockSpec(shape, lambda *_: (0,) * nd, pipeline_mode=pl.Buffered(1))


def _rmsnorm(x, g):
    return x * lax.rsqrt(jnp.mean(x * x, axis=-1, keepdims=True) + NORM_EPS) * g


def _dot(a, b):
    return jnp.dot(a, b, preferred_element_type=F32)


def _dot_nt(a, b):
    return lax.dot_general(a, b, (((1,), (1,)), ((), ())), preferred_element_type=F32)


def _dot_tn(a, b):
    return lax.dot_general(a, b, (((0,), (0,)), ((), ())), preferred_element_type=F32)


def _split_bf16(x, terms):
    out = []
    for _ in range(terms - 1):
        hi = x.astype(BF16)
        out.append(hi)
        x = x - hi.astype(F32)
    out.append(x.astype(BF16))
    return out


def _head_sum(x, ones_bd, terms):
    w = x.shape[-1]
    ones = ones_bd[:w, :w]
    acc = None
    for t in _split_bf16(x, terms):
        y = _dot(t, ones)
        acc = y if acc is None else acc + y
    return acc


def _ffn_body(x_ref, g_ref, wup_ref, wdn_ref, gf_ref, o_ref, act_ref, *, final_norm):
    x = x_ref[...]
    h = _rmsnorm(x, g_ref[...]).astype(BF16)
    for c in range(D_FF // FF_CHUNK):
        lo, hi = c * FF_CHUNK, (c + 1) * FF_CHUNK
        gate = _dot(h, wup_ref[:, lo:hi])
        up = _dot(h, wup_ref[:, D_FF + lo:D_FF + hi])
        act_ref[:, lo:hi] = (gate * jax.nn.sigmoid(gate) * up).astype(BF16)
    y = x + 0.5 * _dot(act_ref[...], wdn_ref[...])
    if final_norm:
        y = _rmsnorm(y, gf_ref[...])
    o_ref[...] = y


def _ffn(x, g, w_up, w_down, g_final, final_norm):
    t, d = x.shape
    tm = min(TOKEN_TILE, t)
    row = pl.BlockSpec((tm, d), lambda i: (i, 0))
    return pl.pallas_call(
        functools.partial(_ffn_body, final_norm=final_norm),
        grid=(t // tm,),
        in_specs=[row, _resident((1, d)), _resident(w_up.shape), _resident(w_down.shape), _resident((1, d))],
        out_specs=row,
        out_shape=jax.ShapeDtypeStruct((t, d), F32),
        scratch_shapes=[pltpu.VMEM((tm, D_FF), BF16)],
        compiler_params=_params("parallel"),
        name="ffn_final" if final_norm else "ffn",
    )(x, g, w_up, w_down, g_final)


def _rope(t, cos, sin_signed, half):
    w = t.shape[-1]
    lane = lax.broadcasted_iota(jnp.int32, t.shape, 1)
    first = (lane % (2 * half)) < half
    partner = jnp.where(first, pltpu.roll(t, w - half, 1), pltpu.roll(t, half, 1))
    return t * cos + partner * sin_signed


def _dup_heads(t):
    lane = lax.broadcasted_iota(jnp.int32, t.shape, 1)
    swapped = pltpu.roll(t, HEAD_DIM, 1)
    low = lane < HEAD_DIM
    return jnp.where(low, t, swapped), jnp.where(low, swapped, t)


def _inproj_body(x_ref, g_ref, w_ref, ones_ref, gq_ref, gk_ref, cga_ref, sga_ref, cwa_ref, swa_ref,
                 prw_ref, gaq_ref, gakv_ref, waq_ref, wakv_ref):
    x = x_ref[...]
    h = _rmsnorm(x, g_ref[...]).astype(BF16)
    prw_ref[...] = _dot(h, w_ref[:, :RWKV_COLS])
    ga = _dot(h, w_ref[:, RWKV_COLS:RWKV_COLS + GA_COLS])
    wa = _dot(h, w_ref[:, RWKV_COLS + GA_COLS:MIX_COLS])
    ones = ones_ref[...]

    def head_norm(t, gain):
        ms = _head_sum(t * t, ones, 1) * (1.0 / HEAD_DIM)
        return t * lax.rsqrt(ms + NORM_EPS) * gain

    tile4 = lambda tab: jnp.concatenate([tab] * 4, axis=1)
    cga, sga = cga_ref[...], sga_ref[...]
    q = _rope(head_norm(ga[:, :512], gq_ref[...]), tile4(cga), tile4(sga), 16)
    k = _rope(head_norm(ga[:, 512:640], gk_ref[:, :LANES]), cga, sga, 16)
    gaq_ref[...] = (q * ATTN_SCALE).astype(BF16)
    k0, k1 = _dup_heads(k)
    v0, v1 = _dup_heads(ga[:, 640:768])
    gakv_ref[...] = jnp.concatenate([k0, k1, v0, v1], axis=1).astype(BF16)
    cwa, swa = cwa_ref[...], swa_ref[...]
    q = _rope(wa[:, :512], tile4(cwa), tile4(swa), 32)
    k = _rope(wa[:, 512:640], cwa, swa, 32)
    waq_ref[...] = (q * ATTN_SCALE).astype(BF16)
    k0, k1 = _dup_heads(k)
    v0, v1 = _dup_heads(wa[:, 640:768])
    wakv_ref[...] = jnp.concatenate([k0, k1, v0, v1], axis=1).astype(BF16)


def _inproj(x, g, w, ones_bd, gq, gk, tabs, seq):
    t, d = x.shape
    tm = min(TOKEN_TILE, seq)
    per_seq = seq // tm
    row = lambda w_: pl.BlockSpec((tm, w_), lambda i: (i, 0))
    tab = pl.BlockSpec((tm, LANES), lambda i: (i % per_seq, 0))
    return pl.pallas_call(
        _inproj_body,
        grid=(t // tm,),
        in_specs=[row(d), _resident((1, d)), _resident(w.shape), _resident(ones_bd.shape),
                  _resident(gq.shape), _resident(gk.shape), tab, tab, tab, tab],
        out_specs=[row(RWKV_COLS), row(512), row(512), row(512), row(512)],
        out_shape=[jax.ShapeDtypeStruct((t, RWKV_COLS), F32)] + [jax.ShapeDtypeStruct((t, 512), BF16)] * 4,
        compiler_params=_params("parallel"),
        name="inproj",
    )(x, g, w, ones_bd, gq, gk, *tabs)


def _block_diag4(x):
    head = lax.broadcasted_iota(jnp.int32, x.shape, 1) // HEAD_DIM
    return jnp.concatenate([jnp.where(head == h, x, jnp.zeros_like(x)) for h in range(4)], axis=0)


def _bd16(x):
    return _block_diag4(x).astype(BF16)


def _rwkv_prep_body(p_ref, hp_ref, hn_ref, mu_ref, w0_ref, w2_ref, a0_ref, a2_ref, g2_ref, kk_ref, ka_ref, rk_ref,
                    ones_ref, *out_refs, ts):
    dir_refs = (out_refs[0:8], out_refs[8:16])
    v_ref, bonus_ref, gate_ref = out_refs[16:19]
    i = pl.program_id(1)
    n = pl.num_programs(1)
    p = p_ref[0]
    prev_row = jnp.where(i > 0, hp_ref[0, 7:8, :], 0.0)
    next_row = jnp.where(i < n - 1, hn_ref[0, 0:1, :], 0.0)
    row = lax.broadcasted_iota(jnp.int32, p.shape, 0)
    prev = jnp.where(row == 0, prev_row, pltpu.roll(p, 1, 0))
    nxt = jnp.where(row == ts - 1, next_row, pltpu.roll(p, ts - 1, 0))
    pm = p + mu_ref[...] * (0.5 * (prev + nxt) - p)

    r, k, v = pm[:, 0:512], pm[:, 512:1024], pm[:, 1024:1536]
    wl, al, gl = pm[:, 1536:1664], pm[:, 1664:1792], pm[:, 1792:1920]
    ones = ones_ref[...]

    u = w0_ref[...] + _dot(jnp.tanh(wl).astype(BF16), w2_ref[...])
    logw = (-float(np.exp(-0.5))) * jax.nn.sigmoid(u)
    asig = jax.nn.sigmoid(a0_ref[...] + _dot(al.astype(BF16), a2_ref[...]))
    kkr = k * kk_ref[...]
    kk = kkr * lax.rsqrt(jnp.maximum(_head_sum(kkr * kkr, ones, 1), 1e-24))
    k_a = ka_ref[...]
    kd = [k * (1.0 + (asig[:, d * 512:(d + 1) * 512] - 1.0) * k_a) for d in range(2)]
    bonus_ref[0] = _head_sum(r * rk_ref[...] * (kd[0] + kd[1]), ones, 2) * v
    gate_ref[0] = _dot(jax.nn.sigmoid(gl).astype(BF16), g2_ref[...])
    v_ref[0] = v.astype(BF16)

    rr = lax.broadcasted_iota(jnp.int32, (ts, ts), 0)
    cc = lax.broadcasted_iota(jnp.int32, (ts, ts), 1)
    same = (rr // CHUNK) == (cc // CHUNK)
    tt = lax.broadcasted_iota(jnp.int32, (CHUNK, QUAD), 0)
    ss = lax.broadcasted_iota(jnp.int32, (CHUNK, QUAD), 1) % CHUNK
    eye = jnp.where(tt == ss, 1.0, 0.0)

    scaled = []
    for d in range(2):
        bp_ref, kp_ref, pt_ref = dir_refs[d][5:8]
        rt_ref = dir_refs[d][1]
        before = (cc <= rr) if d == 0 else (cc >= rr)
        lhs = jnp.concatenate([jnp.where(same & before, 1.0, 0.0), jnp.where(same, 1.0, 0.0)], axis=0).astype(BF16)
        lw = logw[:, d * 512:(d + 1) * 512]
        sums = None
        for term in _split_bf16(lw, 3):
            y = _dot(lhs, term)
            sums = y if sums is None else sums + y
        cl, tot = sums[:ts], sums[ts:]
        e_neg = jnp.exp(-cl)
        b_raw = kk * asig[:, d * 512:(d + 1) * 512]
        r_t = r * jnp.exp(cl)
        e_out = jnp.exp(tot - cl)
        bp_ref[0] = (b_raw * e_out).astype(BF16)
        kp_ref[0] = (kd[d] * e_out).astype(BF16)
        rt_ref[0] = r_t.astype(BF16)
        for c in range(ts // CHUNK):
            last = (c + 1) * CHUNK - 1
            pt_ref[0, c] = jnp.exp(tot[last:last + 1, :])
        scaled.append((-kk * jnp.exp(cl - lw), b_raw * e_neg, kd[d] * e_neg, r_t))

    chains = [(d, c, qd) for d in range(2) for c in range(ts // CHUNK) for qd in range(RWKV_DIM // QUAD)]
    cut = lambda arr, c, qd: arr[c * CHUNK:(c + 1) * CHUNK, qd * QUAD:(qd + 1) * QUAD]
    strict = [ss < tt, ss > tt]
    incl = [ss <= tt, ss >= tt]
    at, x2 = {}, {}
    for ch in chains:
        d, c, qd = ch
        at[ch] = cut(scaled[d][0], c, qd)
        x2[ch] = jnp.concatenate([at[ch], cut(scaled[d][3], c, qd)], axis=0).astype(BF16)
    w_b = {ch: _dot_nt(x2[ch], _bd16(cut(scaled[ch[0]][1], ch[1], ch[2]))) for ch in chains}
    w_k = {ch: _dot_nt(x2[ch], _bd16(cut(scaled[ch[0]][2], ch[1], ch[2]))) for ch in chains}
    a_ab, gy = {}, {}
    for ch in chains:
        d, c, qd = ch
        a_ab[ch] = jnp.where(strict[d], w_b[ch][:CHUNK], 0.0)
        dir_refs[d][2][0, c * CHUNK:(c + 1) * CHUNK, qd * QUAD:(qd + 1) * QUAD] = (
            jnp.where(incl[d], w_b[ch][CHUNK:], 0.0).astype(BF16))
        lk = jnp.concatenate([jnp.where(strict[d], w_k[ch][:CHUNK], 0.0),
                              jnp.where(incl[d], w_k[ch][CHUNK:], 0.0)], axis=0)
        gy[ch] = _dot(lk.astype(BF16), _bd16(cut(v, c, qd)))
    tm = {ch: eye + a_ab[ch] for ch in chains}
    pw = {ch: _dot(a_ab[ch].astype(BF16), _bd16(a_ab[ch])) for ch in chains}
    for _ in range(4):
        both = {ch: _dot(jnp.concatenate([pw[ch], tm[ch]], axis=0).astype(BF16), _bd16(pw[ch])) for ch in chains}
        tm = {ch: tm[ch] + both[ch][CHUNK:] for ch in chains}
        pw = {ch: both[ch][:CHUNK] for ch in chains}
    tm = {ch: (tm[ch] + _dot(tm[ch].astype(BF16), _bd16(pw[ch]))).astype(BF16) for ch in chains}
    for ch in chains:
        d, c, qd = ch
        rs, ls = slice(c * CHUNK, (c + 1) * CHUNK), slice(qd * QUAD, (qd + 1) * QUAD)
        at_ref, _, _, uv_ref, yv_ref = dir_refs[d][0:5]
        uv_ref[0, rs, ls] = _dot(tm[ch], _bd16(gy[ch][:CHUNK]))
        at_ref[0, rs, ls] = _dot(tm[ch], _bd16(at[ch])).astype(BF16)
        yv_ref[0, rs, ls] = gy[ch][CHUNK:]


def _rwkv_prep(p, mu, w0, w2bd, a0, a2bd, g2, k_k, k_a, r_k, ones_bd):
    b, s, _ = p.shape
    ts = min(PREP_TILE, s)
    nt = s // ts
    h8 = ts // 8
    main = pl.BlockSpec((1, ts, RWKV_COLS), lambda bi, i: (bi, i, 0))
    halo_prev = pl.BlockSpec((1, 8, RWKV_COLS), lambda bi, i: (bi, jnp.maximum(i * h8 - 1, 0), 0))
    halo_next = pl.BlockSpec((1, 8, RWKV_COLS), lambda bi, i: (bi, jnp.minimum((i + 1) * h8, s // 8 - 1), 0))
    tok = pl.BlockSpec((1, ts, RWKV_DIM), lambda bi, i: (bi, i, 0))
    ptot = pl.BlockSpec((1, ts // CHUNK, 1, RWKV_DIM), lambda bi, i: (bi, i, 0, 0))
    tok16 = jax.ShapeDtypeStruct((b, s, RWKV_DIM), BF16)
    tok32 = jax.ShapeDtypeStruct((b, s, RWKV_DIM), F32)
    pt32 = jax.ShapeDtypeStruct((b, s // CHUNK, 1, RWKV_DIM), F32)
    per_dir_specs = [tok, tok, tok, tok, tok, tok, tok, ptot]
    per_dir_shapes = [tok16, tok16, tok16, tok32, tok32, tok16, tok16, pt32]
    consts = [mu, w0, w2bd, a0, a2bd, g2, k_k, k_a, r_k, ones_bd]
    return pl.pallas_call(
        functools.partial(_rwkv_prep_body, ts=ts),
        grid=(b, nt),
        in_specs=[main, halo_prev, halo_next] + [_resident(c.shape) for c in consts],
        out_specs=per_dir_specs * 2 + [tok, tok, tok],
        out_shape=per_dir_shapes * 2 + [tok16, tok32, tok32],
        compiler_params=_params("parallel", "parallel"),
        name="rwkv_prep",
    )(p, p, p, *consts)


def _rwkv_scan_body(*refs, nc):
    fwd, bwd = refs[0:8], refs[8:16]
    vf_ref, vb_ref = refs[16:18]
    yf_ref, yb_ref = refs[18:20]
    st_ref = refs[20]
    step = pl.program_id(1)

    @pl.when(step == 0)
    def _():
        st_ref[...] = jnp.zeros_like(st_ref)

    rr = lax.broadcasted_iota(jnp.int32, (QUAD, QUAD), 0) // HEAD_DIM
    cc = lax.broadcasted_iota(jnp.int32, (QUAD, QUAD), 1) // HEAD_DIM
    diag = rr == cc
    dirs = ((fwd, vf_ref, yf_ref), (bwd, vb_ref, yb_ref))
    chains = [(d, qd) for d in range(2) for qd in range(RWKV_DIM // QUAD)]
    st = {ch: st_ref[ch[0], ch[1]] for ch in chains}
    for j in range(nc):
        z, u = {}, {}
        for ch in chains:
            d, qd = ch
            c = j if d == 0 else nc - 1 - j
            rs, ls = slice(c * CHUNK, (c + 1) * CHUNK), slice(qd * QUAD, (qd + 1) * QUAD)
            at_ref, rt_ref = dirs[d][0][0:2]
            x2 = jnp.concatenate([at_ref[0, rs, ls], rt_ref[0, rs, ls]], axis=0)
            z[ch] = _dot_nt(x2, st[ch].astype(BF16))
        for ch in chains:
            d, qd = ch
            c = j if d == 0 else nc - 1 - j
            rs, ls = slice(c * CHUNK, (c + 1) * CHUNK), slice(qd * QUAD, (qd + 1) * QUAD)
            (_, _, mrb_ref, uv_ref, yv_ref, bp_ref, kp_ref, pt_ref), v_ref, y_ref = dirs[d]
            u = z[ch][:CHUNK] + uv_ref[0, rs, ls]
            y_ref[0, rs, ls] = z[ch][CHUNK:] + _dot(mrb_ref[0, rs, ls], _bd16(u)) + yv_ref[0, rs, ls]
            uv2 = jnp.concatenate([u.astype(BF16), v_ref[0, rs, ls]], axis=0)
            bk2 = jnp.concatenate([bp_ref[0, rs, ls], kp_ref[0, rs, ls]], axis=0)
            st[ch] = st[ch] * pt_ref[0, c][:, ls] + jnp.where(diag, _dot_tn(uv2, bk2), 0.0)
    for ch in chains:
        st_ref[ch[0], ch[1]] = st[ch]


def _rwkv_scan(fwd_arrs, bwd_arrs, v16):
    b, s, _ = v16.shape
    nc = min(SCAN_CHUNKS, s // CHUNK)
    rows = nc * CHUNK
    steps = s // rows
    f_tok = pl.BlockSpec((1, rows, RWKV_DIM), lambda bi, i: (bi, i, 0))
    b_tok = pl.BlockSpec((1, rows, RWKV_DIM), lambda bi, i: (bi, steps - 1 - i, 0))
    f_pt = pl.BlockSpec((1, nc, 1, RWKV_DIM), lambda bi, i: (bi, i, 0, 0))
    b_pt = pl.BlockSpec((1, nc, 1, RWKV_DIM), lambda bi, i: (bi, steps - 1 - i, 0, 0))
    y_shape = jax.ShapeDtypeStruct((b, s, RWKV_DIM), F32)
    return pl.pallas_call(
        functools.partial(_rwkv_scan_body, nc=nc),
        grid=(b, steps),
        in_specs=[f_tok] * 7 + [f_pt] + [b_tok] * 7 + [b_pt] + [f_tok, b_tok],
        out_specs=[f_tok, b_tok],
        out_shape=[y_shape, y_shape],
        scratch_shapes=[pltpu.VMEM((2, RWKV_DIM // QUAD, QUAD, QUAD), F32)],
        compiler_params=_params("parallel", "arbitrary"),
        name="rwkv_scan",
    )(*fwd_arrs, *bwd_arrs, v16, v16)


def _rwkv_post_body(yf_ref, yb_ref, bonus_ref, gate_ref, lg_ref, lb_ref, ones_ref, o_ref):
    ones = ones_ref[...]
    wkv = yf_ref[...] + yb_ref[...]
    mean = _head_sum(wkv, ones, 3) * (1.0 / HEAD_DIM)
    dev = wkv - mean
    var = _head_sum(dev * dev, ones, 2) * (1.0 / HEAD_DIM)
    y = dev * lax.rsqrt(var + LNX_EPS) * lg_ref[...] + lb_ref[...] + bonus_ref[...]
    o_ref[...] = (y * gate_ref[...]).astype(BF16)


def _rwkv_post(yf, yb, bonus, gate, lnx_g, lnx_b, ones_bd):
    t, w = yf.shape
    tm = min(TOKEN_TILE, t)
    row = pl.BlockSpec((tm, w), lambda i: (i, 0))
    return pl.pallas_call(
        _rwkv_post_body,
        grid=(t // tm,),
        in_specs=[row, row, row, row, _resident((1, w)), _resident((1, w)), _resident(ones_bd.shape)],
        out_specs=row,
        out_shape=jax.ShapeDtypeStruct((t, w), BF16),
        compiler_params=_params("parallel"),
        name="rwkv_post",
    )(yf, yb, bonus, gate, lnx_g, lnx_b, ones_bd)


def _group_queries(q_ref, g, tq):
    lane = lax.broadcasted_iota(jnp.int32, (tq, LANES), 1)
    parts = []
    for pair in range(2):
        qp = q_ref[0, :, (2 * g + pair) * LANES:(2 * g + pair + 1) * LANES]
        parts.append(jnp.where(lane < HEAD_DIM, qp, jnp.zeros_like(qp)))
        parts.append(jnp.where(lane >= HEAD_DIM, qp, jnp.zeros_like(qp)))
    return jnp.concatenate(parts, axis=0)


def _ungroup(o, tq):
    lane = lax.broadcasted_iota(jnp.int32, (tq, LANES), 1)
    low = lane < HEAD_DIM
    return jnp.concatenate([jnp.where(low, o[0:tq], o[tq:2 * tq]),
                            jnp.where(low, o[2 * tq:3 * tq], o[3 * tq:4 * tq])], axis=1)


def _ga_body(q_ref, kv_ref, o_ref, *, tq):
    for g in range(2):
        kd = kv_ref[0, :, g * LANES:(g + 1) * LANES]
        vd = kv_ref[0, :, (2 + g) * LANES:(3 + g) * LANES]
        sc = _dot_nt(_group_queries(q_ref, g, tq), kd)
        e = jnp.exp(sc - jnp.max(sc, axis=-1, keepdims=True))
        o = _dot(e.astype(BF16), vd) / jnp.sum(e, axis=-1, keepdims=True)
        o_ref[0, :, g * 2 * LANES:(g + 1) * 2 * LANES] = _ungroup(o, tq).astype(BF16)


def _global_attention(q, kv):
    b, s, w = q.shape
    tq = min(GA_Q_TILE, s)
    qs = pl.BlockSpec((1, tq, w), lambda bi, i: (bi, i, 0))
    kvs = pl.BlockSpec((1, s, w), lambda bi, i: (bi, 0, 0))
    return pl.pallas_call(
        functools.partial(_ga_body, tq=tq),
        grid=(b, s // tq),
        in_specs=[qs, kvs],
        out_specs=qs,
        out_shape=jax.ShapeDtypeStruct((b, s, w), BF16),
        compiler_params=_params("parallel", "arbitrary"),
        name="global_attn",
    )(q, kv)


def _wa_body(sink_ref, q_ref, kv_ref, o_ref, *, tq, span, seq):
    i = pl.program_id(1)
    start = pl.multiple_of(jnp.clip(i * tq - WINDOW, 0, seq - span), WINDOW)
    rows = lax.broadcasted_iota(jnp.int32, (4 * tq, span), 0)
    cols = lax.broadcasted_iota(jnp.int32, (4 * tq, span), 1)
    rel = (start + cols) - (i * tq + rows % tq)
    valid = jnp.abs(rel) <= WINDOW
    for g in range(2):
        kd = kv_ref[0, pl.ds(start, span), g * LANES:(g + 1) * LANES]
        vd = kv_ref[0, pl.ds(start, span), (2 + g) * LANES:(3 + g) * LANES]
        sc = jnp.where(valid, _dot_nt(_group_queries(q_ref, g, tq), kd), NEG_INF)
        sink = jnp.concatenate([jnp.full((tq, 1), sink_ref[4 * g + h], F32) for h in range(4)], axis=0)
        m = jnp.maximum(jnp.max(sc, axis=-1, keepdims=True), sink)
        e = jnp.exp(sc - m)
        denom = jnp.sum(e, axis=-1, keepdims=True) + jnp.exp(sink - m)
        o = _dot(e.astype(BF16), vd) / denom
        o_ref[0, :, g * 2 * LANES:(g + 1) * 2 * LANES] = _ungroup(o, tq).astype(BF16)


def _window_attention(q, kv, sink):
    b, s, w = q.shape
    tq = min(WA_Q_TILE, s)
    span = min(tq + 2 * WINDOW, s)
    qs = pl.BlockSpec((1, tq, w), lambda bi, i, *_: (bi, i, 0))
    kvs = pl.BlockSpec((1, s, w), lambda bi, i, *_: (bi, 0, 0))
    return pl.pallas_call(
        functools.partial(_wa_body, tq=tq, span=span, seq=s),
        grid_spec=pltpu.PrefetchScalarGridSpec(
            num_scalar_prefetch=1, grid=(b, s // tq), in_specs=[qs, kvs], out_specs=qs),
        out_shape=jax.ShapeDtypeStruct((b, s, w), BF16),
        compiler_params=_params("parallel", "arbitrary"),
        name="window_attn",
    )(sink, q, kv)


def _merge_body(x_ref, g_ref, wg_ref, ya_ref, yb_ref, yc_ref, woa_ref, wob_ref, woc_ref, wout_ref, o_ref):
    x = x_ref[...]
    h = _rmsnorm(x, g_ref[...]).astype(BF16)
    merged = None
    for j, (y_ref, wo_ref) in enumerate(((ya_ref, woa_ref), (yb_ref, wob_ref), (yc_ref, woc_ref))):
        gate = jax.nn.sigmoid(_dot(h, wg_ref[:, j * D_MODEL:(j + 1) * D_MODEL]))
        term = gate * _dot(y_ref[...], wo_ref[...])
        merged = term if merged is None else merged + term
    o_ref[...] = x + _dot(merged.astype(BF16), wout_ref[...])


def _merge(x, g, wg, ya, yb, yc, woa, wob, woc, wout):
    t, d = x.shape
    tm = min(TOKEN_TILE, t)
    row = lambda w_: pl.BlockSpec((tm, w_), lambda i: (i, 0))
    return pl.pallas_call(
        _merge_body,
        grid=(t // tm,),
        in_specs=[row(d), _resident((1, d)), _resident(wg.shape), row(512), row(512), row(512),
                  _resident(woa.shape), _resident(wob.shape), _resident(woc.shape), _resident(wout.shape)],
        out_specs=row(d),
        out_shape=jax.ShapeDtypeStruct((t, d), F32),
        compiler_params=_params("parallel"),
        name="merge",
    )(x, g, wg, ya, yb, yc, woa, wob, woc, wout)


def _rope_tables(seq):
    t = jnp.arange(seq)

    def angles(pos, dim):
        inv_freq = ROPE_THETA ** (-jnp.arange(0, dim, 2, dtype=F32) / dim)
        return pos.astype(F32)[:, None] * inv_freq[None, :]

    ang_row, ang_col = angles(t // GRID_W, HEAD_DIM // 2), angles(t % GRID_W, HEAD_DIM // 2)
    ang_1d = angles(t, HEAD_DIM)
    ga_cos = jnp.concatenate([jnp.cos(ang_row)] * 2 + [jnp.cos(ang_col)] * 2, axis=1)
    ga_sin = jnp.concatenate([-jnp.sin(ang_row), jnp.sin(ang_row), -jnp.sin(ang_col), jnp.sin(ang_col)], axis=1)
    wa_cos = jnp.concatenate([jnp.cos(ang_1d)] * 2, axis=1)
    wa_sin = jnp.concatenate([-jnp.sin(ang_1d), jnp.sin(ang_1d)], axis=1)
    return tuple(jnp.concatenate([a, a], axis=1) for a in (ga_cos, ga_sin, wa_cos, wa_sin))


def _two_way(w):
    z = jnp.zeros_like(w[0])
    return jnp.concatenate([jnp.concatenate([w[0], z], axis=1), jnp.concatenate([z, w[1]], axis=1)], axis=0)


def kernel(x, ffn1_norm, ffn1_w_up, ffn1_w_down, mix_norm, w_in, rwkv_mu, rwkv_w0, rwkv_w2, rwkv_a0, rwkv_a2, rwkv_g2, rwkv_k_k, rwkv_k_a, rwkv_r_k, rwkv_lnx_g, rwkv_lnx_b, ga_q_norm, ga_k_norm, wa_sink, w_o_rwkv, w_o_ga, w_o_wa, w_out, ffn2_norm, ffn2_w_up, ffn2_w_down, final_norm):
    bsz, seq, dm = x.shape
    depth = w_in.shape[0]
    tokens = bsz * seq
    tabs = _rope_tables(seq)
    idx = jnp.arange(512)
    ones_bd = (idx[:, None] // HEAD_DIM == idx[None, :] // HEAD_DIM).astype(BF16)
    row1 = lambda a: a.reshape(1, -1)
    final_g = row1(final_norm)

    xt = x.reshape(tokens, dm)
    for l in range(depth):
        xt = _ffn(xt, row1(ffn1_norm[l]), ffn1_w_up[l].astype(BF16), ffn1_w_down[l].astype(BF16), final_g, False)

        w_mix = w_in[l, :, :MIX_COLS].astype(BF16)
        w_gate = w_in[l, :, MIX_COLS:].astype(BF16)
        p_rwkv, ga_q, ga_kv, wa_q, wa_kv = _inproj(
            xt, row1(mix_norm[l]), w_mix, ones_bd, row1(jnp.tile(ga_q_norm[l], 8)), row1(jnp.tile(ga_k_norm[l], 8)),
            tabs, seq)

        outs = _rwkv_prep(
            p_rwkv.reshape(bsz, seq, RWKV_COLS), row1(rwkv_mu[l]), row1(rwkv_w0[l]), _two_way(rwkv_w2[l]).astype(BF16),
            row1(rwkv_a0[l]), _two_way(rwkv_a2[l]).astype(BF16), rwkv_g2[l].astype(BF16), row1(rwkv_k_k[l]),
            row1(rwkv_k_a[l]), row1(rwkv_r_k[l]), ones_bd)
        v16, bonus, gate = outs[16:19]
        y_f, y_b = _rwkv_scan(outs[0:8], outs[8:16], v16)
        flat = lambda a: a.reshape(tokens, RWKV_DIM)
        y_a = _rwkv_post(flat(y_f), flat(y_b), flat(bonus), flat(gate), row1(rwkv_lnx_g[l]), row1(rwkv_lnx_b[l]), ones_bd)

        y_g = _global_attention(ga_q.reshape(bsz, seq, 512), ga_kv.reshape(bsz, seq, 512))
        y_w = _window_attention(wa_q.reshape(bsz, seq, 512), wa_kv.reshape(bsz, seq, 512), wa_sink[l])

        xt = _merge(xt, row1(mix_norm[l]), w_gate, y_a, flat(y_g), flat(y_w), w_o_rwkv[l].astype(BF16),
                    w_o_ga[l].astype(BF16), w_o_wa[l].astype(BF16), w_out[l].astype(BF16))

        xt = _ffn(xt, row1(ffn2_norm[l]), ffn2_w_up[l].astype(BF16), ffn2_w_down[l].astype(BF16), final_g,
                  l == depth - 1)
    return xt.reshape(bsz, seq, dm)
```

```python
import functools

import jax
import jax.numpy as jnp
import numpy as np
from jax import lax
from jax.experimental import pallas as pl
from jax.experimental.pallas import tpu as pltpu

F32 = jnp.float32
BF16 = jnp.bfloat16

D_MODEL = 1024
HEAD_DIM = 64
RWKV_DIM = 512
RWKV_COLS = 1920
GA_COLS = 768
WA_COLS = 768
MIX_COLS = RWKV_COLS + GA_COLS + WA_COLS
D_FF = 2816
GRID_W = 64
WINDOW = 128
ROPE_THETA = 10000.0
NORM_EPS = 1e-6
LNX_EPS = HEAD_DIM * 1e-5
NEG_INF = -1e30
ATTN_SCALE = HEAD_DIM ** -0.5

CHUNK = 64
QUAD = 4 * HEAD_DIM
LANES = 128
VMEM_LIMIT = 56 * 1024 * 1024

TOKEN_TILE = 512
FF_CHUNK = 1408
PREP_TILE = 256
PREP_SUB = 128
SCAN_CHUNKS = 4
GA_Q_TILE = 256
WA_Q_TILE = 256


def _params(*sem):
    return pltpu.CompilerParams(dimension_semantics=sem, vmem_limit_bytes=VMEM_LIMIT)


def _resident(shape):
    nd = len(shape)
    return pl.BlockSpec(shape, lambda *_: (0,) * nd, pipeline_mode=pl.Buffered(1))


def _rmsnorm(x, g):
    return x * lax.rsqrt(jnp.mean(x * x, axis=-1, keepdims=True) + NORM_EPS) * g


def _dot(a, b):
    return jnp.dot(a, b, preferred_element_type=F32)


def _dot_nt(a, b):
    return lax.dot_general(a, b, (((1,), (1,)), ((), ())), preferred_element_type=F32)


def _dot_tn(a, b):
    return lax.dot_general(a, b, (((0,), (0,)), ((), ())), preferred_element_type=F32)


def _split_bf16(x, terms):
    out = []
    for _ in range(terms - 1):
        hi = x.astype(BF16)
        out.append(hi)
        x = x - hi.astype(F32)
    out.append(x.astype(BF16))
    return out


def _head_sum(x, ones_bd, terms):
    w = x.shape[-1]
    blk = min(w, QUAD)
    ones = ones_bd[:blk, :blk]
    parts = []
    for c0 in range(0, w, blk):
        acc = None
        for t in _split_bf16(x[:, c0:c0 + blk], terms):
            y = _dot(t, ones)
            acc = y if acc is None else acc + y
        parts.append(acc)
    return parts[0] if len(parts) == 1 else jnp.concatenate(parts, axis=1)


def _ffn_body(x_ref, g_ref, wup_ref, wdn_ref, gf_ref, o_ref, act_ref, *, final_norm):
    x = x_ref[...]
    h = _rmsnorm(x, g_ref[...]).astype(BF16)
    for c in range(D_FF // FF_CHUNK):
        lo, hi = c * FF_CHUNK, (c + 1) * FF_CHUNK
        gate = _dot(h, wup_ref[:, lo:hi])
        up = _dot(h, wup_ref[:, D_FF + lo:D_FF + hi])
        act_ref[:, lo:hi] = (gate * jax.nn.sigmoid(gate) * up).astype(BF16)
    y = x + 0.5 * _dot(act_ref[...], wdn_ref[...])
    if final_norm:
        y = _rmsnorm(y, gf_ref[...])
    o_ref[...] = y


def _ffn(x, g, w_up, w_down, g_final, final_norm):
    t, d = x.shape
    tm = min(TOKEN_TILE, t)
    row = pl.BlockSpec((tm, d), lambda i: (i, 0))
    return pl.pallas_call(
        functools.partial(_ffn_body, final_norm=final_norm),
        grid=(t // tm,),
        in_specs=[row, _resident((1, d)), _resident(w_up.shape), _resident(w_down.shape), _resident((1, d))],
        out_specs=row,
        out_shape=jax.ShapeDtypeStruct((t, d), F32),
        scratch_shapes=[pltpu.VMEM((tm, D_FF), BF16)],
        compiler_params=_params("parallel"),
        name="ffn_final" if final_norm else "ffn",
    )(x, g, w_up, w_down, g_final)


def _rope(t, cos, sin_signed, half):
    w = t.shape[-1]
    lane = lax.broadcasted_iota(jnp.int32, t.shape, 1)
    first = (lane % (2 * half)) < half
    partner = jnp.where(first, pltpu.roll(t, w - half, 1), pltpu.roll(t, half, 1))
    return t * cos + partner * sin_signed


def _dup_heads(t):
    lane = lax.broadcasted_iota(jnp.int32, t.shape, 1)
    swapped = pltpu.roll(t, HEAD_DIM, 1)
    low = lane < HEAD_DIM
    return jnp.where(low, t, swapped), jnp.where(low, swapped, t)


def _inproj_body(x_ref, g_ref, w_ref, ones_ref, gq_ref, gk_ref, cga_ref, sga_ref, cwa_ref, swa_ref,
                 prw_ref, gaq_ref, gakv_ref, waq_ref, wakv_ref):
    x = x_ref[...]
    h = _rmsnorm(x, g_ref[...]).astype(BF16)
    prw_ref[...] = _dot(h, w_ref[:, :RWKV_COLS])
    ga = _dot(h, w_ref[:, RWKV_COLS:RWKV_COLS + GA_COLS])
    wa = _dot(h, w_ref[:, RWKV_COLS + GA_COLS:MIX_COLS])
    ones = ones_ref[...]

    def head_norm(t, gain):
        ms = _head_sum(t * t, ones, 1) * (1.0 / HEAD_DIM)
        return t * lax.rsqrt(ms + NORM_EPS) * gain

    tile4 = lambda tab: jnp.concatenate([tab] * 4, axis=1)
    cga, sga = cga_ref[...], sga_ref[...]
    q = _rope(head_norm(ga[:, :512], gq_ref[...]), tile4(cga), tile4(sga), 16)
    k = _rope(head_norm(ga[:, 512:640], gk_ref[:, :LANES]), cga, sga, 16)
    gaq_ref[...] = (q * ATTN_SCALE).astype(BF16)
    k0, k1 = _dup_heads(k)
    v0, v1 = _dup_heads(ga[:, 640:768])
    gakv_ref[...] = jnp.concatenate([k0, k1, v0, v1], axis=1).astype(BF16)
    cwa, swa = cwa_ref[...], swa_ref[...]
    q = _rope(wa[:, :512], tile4(cwa), tile4(swa), 32)
    k = _rope(wa[:, 512:640], cwa, swa, 32)
    waq_ref[...] = (q * ATTN_SCALE).astype(BF16)
    k0, k1 = _dup_heads(k)
    v0, v1 = _dup_heads(wa[:, 640:768])
    wakv_ref[...] = jnp.concatenate([k0, k1, v0, v1], axis=1).astype(BF16)


def _inproj(x, g, w, ones_bd, gq, gk, tabs, seq):
    t, d = x.shape
    tm = min(TOKEN_TILE, seq)
    per_seq = seq // tm
    row = lambda w_: pl.BlockSpec((tm, w_), lambda i: (i, 0))
    tab = pl.BlockSpec((tm, LANES), lambda i: (i % per_seq, 0))
    return pl.pallas_call(
        _inproj_body,
        grid=(t // tm,),
        in_specs=[row(d), _resident((1, d)), _resident(w.shape), _resident(ones_bd.shape),
                  _resident(gq.shape), _resident(gk.shape), tab, tab, tab, tab],
        out_specs=[row(RWKV_COLS), row(512), row(512), row(512), row(512)],
        out_shape=[jax.ShapeDtypeStruct((t, RWKV_COLS), F32)] + [jax.ShapeDtypeStruct((t, 512), BF16)] * 4,
        compiler_params=_params("parallel"),
        name="inproj",
    )(x, g, w, ones_bd, gq, gk, *tabs)


def _block_diag4(x):
    head = lax.broadcasted_iota(jnp.int32, x.shape, 1) // HEAD_DIM
    return jnp.concatenate([jnp.where(head == h, x, jnp.zeros_like(x)) for h in range(4)], axis=0)


def _bd16(x):
    return _block_diag4(x).astype(BF16)


def _rwkv_prep_body(p_ref, hp_ref, hn_ref, mu_ref, w0_ref, w2_ref, a0_ref, a2_ref, g2_ref, kk_ref, ka_ref, rk_ref,
                    ones_ref, *out_refs, ts):
    i = pl.program_id(1)
    n = pl.num_programs(1)
    edge_prev = jnp.where(i > 0, hp_ref[0, 7:8, :], 0.0)
    edge_next = jnp.where(i < n - 1, hn_ref[0, 0:1, :], 0.0)
    consts = (mu_ref, w0_ref, w2_ref, a0_ref, a2_ref, g2_ref, kk_ref, ka_ref, rk_ref, ones_ref)
    for r0 in range(0, ts, PREP_SUB):
        prev_row = edge_prev if r0 == 0 else p_ref[0, r0 - 1:r0, :]
        next_row = edge_next if r0 + PREP_SUB == ts else p_ref[0, r0 + PREP_SUB:r0 + PREP_SUB + 1, :]
        _rwkv_prep_rows(p_ref[0, r0:r0 + PREP_SUB, :], prev_row, next_row, consts, out_refs, r0, PREP_SUB)


def _rwkv_prep_rows(p, prev_row, next_row, consts, out_refs, r0, ts):
    mu_ref, w0_ref, w2_ref, a0_ref, a2_ref, g2_ref, kk_ref, ka_ref, rk_ref, ones_ref = consts
    dir_refs = (out_refs[0:8], out_refs[8:16])
    v_ref, bonus_ref, gate_ref = out_refs[16:19]
    rows = slice(r0, r0 + ts)
    row = lax.broadcasted_iota(jnp.int32, p.shape, 0)
    prev = jnp.where(row == 0, prev_row, pltpu.roll(p, 1, 0))
    nxt = jnp.where(row == ts - 1, next_row, pltpu.roll(p, ts - 1, 0))
    pm = p + mu_ref[...] * (0.5 * (prev + nxt) - p)

    r, k, v = pm[:, 0:512], pm[:, 512:1024], pm[:, 1024:1536]
    wl, al, gl = pm[:, 1536:1664], pm[:, 1664:1792], pm[:, 1792:1920]
    ones = ones_ref[...]

    u = w0_ref[...] + _dot(jnp.tanh(wl).astype(BF16), w2_ref[...])
    logw = (-float(np.exp(-0.5))) * jax.nn.sigmoid(u)
    asig = jax.nn.sigmoid(a0_ref[...] + _dot(al.astype(BF16), a2_ref[...]))
    kkr = k * kk_ref[...]
    kk = kkr * lax.rsqrt(jnp.maximum(_head_sum(kkr * kkr, ones, 1), 1e-24))
    k_a = ka_ref[...]
    kd = [k * (1.0 + (asig[:, d * 512:(d + 1) * 512] - 1.0) * k_a) for d in range(2)]
    bonus_ref[0, rows] = _head_sum(r * rk_ref[...] * (kd[0] + kd[1]), ones, 1) * v
    gate_ref[0, rows] = _dot(jax.nn.sigmoid(gl).astype(BF16), g2_ref[...])
    v_ref[0, rows] = v.astype(BF16)

    rr = lax.broadcasted_iota(jnp.int32, (ts, ts), 0)
    cc = lax.broadcasted_iota(jnp.int32, (ts, ts), 1)
    same = (rr // CHUNK) == (cc // CHUNK)
    tt = lax.broadcasted_iota(jnp.int32, (CHUNK, QUAD), 0)
    ss = lax.broadcasted_iota(jnp.int32, (CHUNK, QUAD), 1) % CHUNK
    eye = jnp.where(tt == ss, 1.0, 0.0)

    scaled = []
    for d in range(2):
        bp_ref, kp_ref, pt_ref = dir_refs[d][5:8]
        rt_ref = dir_refs[d][1]
        before = (cc <= rr) if d == 0 else (cc >= rr)
        lhs = jnp.where(same & before, 1.0, 0.0).astype(BF16)
        lw = logw[:, d * 512:(d + 1) * 512]
        cl = None
        for term in _split_bf16(lw, 2):
            y = _dot(lhs, term)
            cl = y if cl is None else cl + y
        ends = [c * CHUNK + (CHUNK - 1 if d == 0 else 0) for c in range(ts // CHUNK)]
        tot = jnp.concatenate([jnp.broadcast_to(cl[e:e + 1, :], (CHUNK, RWKV_DIM)) for e in ends], axis=0)
        e_neg = jnp.exp(-cl)
        b_raw = kk * asig[:, d * 512:(d + 1) * 512]
        r_t = r * jnp.exp(cl)
        e_out = jnp.exp(tot - cl)
        bp_ref[0, rows] = (b_raw * e_out).astype(BF16)
        kp_ref[0, rows] = (kd[d] * e_out).astype(BF16)
        rt_ref[0, rows] = r_t.astype(BF16)
        for c, e in enumerate(ends):
            pt_ref[0, r0 // CHUNK + c] = jnp.exp(cl[e:e + 1, :])
        scaled.append((-kk * jnp.exp(cl - lw), b_raw * e_neg, kd[d] * e_neg, r_t))

    chains = [(d, c, qd) for d in range(2) for c in range(ts // CHUNK) for qd in range(RWKV_DIM // QUAD)]
    cut = lambda arr, c, qd: arr[c * CHUNK:(c + 1) * CHUNK, qd * QUAD:(qd + 1) * QUAD]
    strict = [ss < tt, ss > tt]
    incl = [ss <= tt, ss >= tt]
    at, x2 = {}, {}
    for ch in chains:
        d, c, qd = ch
        at[ch] = cut(scaled[d][0], c, qd)
        x2[ch] = jnp.concatenate([at[ch], cut(scaled[d][3], c, qd)], axis=0).astype(BF16)
    w_b = {ch: _dot_nt(x2[ch], _bd16(cut(scaled[ch[0]][1], ch[1], ch[2]))) for ch in chains}
    w_k = {ch: _dot_nt(x2[ch], _bd16(cut(scaled[ch[0]][2], ch[1], ch[2]))) for ch in chains}
    a_ab, gy = {}, {}
    for ch in chains:
        d, c, qd = ch
        a_ab[ch] = jnp.where(strict[d], w_b[ch][:CHUNK], 0.0)
        dir_refs[d][2][0, r0 + c * CHUNK:r0 + (c + 1) * CHUNK, qd * QUAD:(qd + 1) * QUAD] = (
            jnp.where(incl[d], w_b[ch][CHUNK:], 0.0).astype(BF16))
        lk = jnp.concatenate([jnp.where(strict[d], w_k[ch][:CHUNK], 0.0),
                              jnp.where(incl[d], w_k[ch][CHUNK:], 0.0)], axis=0)
        gy[ch] = _dot(lk.astype(BF16), _bd16(cut(v, c, qd)))
    tm = {ch: eye + a_ab[ch] for ch in chains}
    pw = {ch: _dot(a_ab[ch].astype(BF16), _bd16(a_ab[ch])) for ch in chains}
    for _ in range(4):
        both = {ch: _dot(jnp.concatenate([pw[ch], tm[ch]], axis=0).astype(BF16), _bd16(pw[ch])) for ch in chains}
        tm = {ch: tm[ch] + both[ch][CHUNK:] for ch in chains}
        pw = {ch: both[ch][:CHUNK] for ch in chains}
    tm = {ch: (tm[ch] + _dot(tm[ch].astype(BF16), _bd16(pw[ch]))).astype(BF16) for ch in chains}
    for ch in chains:
        d, c, qd = ch
        rs, ls = slice(r0 + c * CHUNK, r0 + (c + 1) * CHUNK), slice(qd * QUAD, (qd + 1) * QUAD)
        at_ref, _, _, uv_ref, yv_ref = dir_refs[d][0:5]
        uv_ref[0, rs, ls] = _dot(tm[ch], _bd16(gy[ch][:CHUNK]))
        at_ref[0, rs, ls] = _dot(tm[ch], _bd16(at[ch])).astype(BF16)
        yv_ref[0, rs, ls] = gy[ch][CHUNK:]


def _rwkv_prep(p, mu, w0, w2bd, a0, a2bd, g2, k_k, k_a, r_k, ones_bd):
    b, s, _ = p.shape
    ts = min(PREP_TILE, s)
    nt = s // ts
    h8 = ts // 8
    main = pl.BlockSpec((1, ts, RWKV_COLS), lambda bi, i: (bi, i, 0))
    halo_prev = pl.BlockSpec((1, 8, RWKV_COLS), lambda bi, i: (bi, jnp.maximum(i * h8 - 1, 0), 0))
    halo_next = pl.BlockSpec((1, 8, RWKV_COLS), lambda bi, i: (bi, jnp.minimum((i + 1) * h8, s // 8 - 1), 0))
    tok = pl.BlockSpec((1, ts, RWKV_DIM), lambda bi, i: (bi, i, 0))
    ptot = pl.BlockSpec((1, ts // CHUNK, 1, RWKV_DIM), lambda bi, i: (bi, i, 0, 0))
    tok16 = jax.ShapeDtypeStruct((b, s, RWKV_DIM), BF16)
    tok32 = jax.ShapeDtypeStruct((b, s, RWKV_DIM), F32)
    pt32 = jax.ShapeDtypeStruct((b, s // CHUNK, 1, RWKV_DIM), F32)
    per_dir_specs = [tok, tok, tok, tok, tok, tok, tok, ptot]
    per_dir_shapes = [tok16, tok16, tok16, tok32, tok32, tok16, tok16, pt32]
    consts = [mu, w0, w2bd, a0, a2bd, g2, k_k, k_a, r_k, ones_bd]
    return pl.pallas_call(
        functools.partial(_rwkv_prep_body, ts=ts),
        grid=(b, nt),
        in_specs=[main, halo_prev, halo_next] + [_resident(c.shape) for c in consts],
        out_specs=per_dir_specs * 2 + [tok, tok, tok],
        out_shape=per_dir_shapes * 2 + [tok16, tok32, tok32],
        compiler_params=_params("parallel", "parallel"),
        name="rwkv_prep",
    )(p, p, p, *consts)


def _rwkv_scan_body(*refs, nc):
    fwd, bwd = refs[0:8], refs[8:16]
    vf_ref, vb_ref = refs[16:18]
    yf_ref, yb_ref = refs[18:20]
    st_ref = refs[20]
    step = pl.program_id(1)

    @pl.when(step == 0)
    def _():
        st_ref[...] = jnp.zeros_like(st_ref)

    rr = lax.broadcasted_iota(jnp.int32, (QUAD, QUAD), 0) // HEAD_DIM
    cc = lax.broadcasted_iota(jnp.int32, (QUAD, QUAD), 1) // HEAD_DIM
    diag = rr == cc
    dirs = ((fwd, vf_ref, yf_ref), (bwd, vb_ref, yb_ref))
    chains = [(d, qd) for d in range(2) for qd in range(RWKV_DIM // QUAD)]
    st = {ch: st_ref[ch[0], ch[1]] for ch in chains}
    for j in range(nc):
        z, u = {}, {}
        for ch in chains:
            d, qd = ch
            c = j if d == 0 else nc - 1 - j
            rs, ls = slice(c * CHUNK, (c + 1) * CHUNK), slice(qd * QUAD, (qd + 1) * QUAD)
            at_ref, rt_ref = dirs[d][0][0:2]
            x2 = jnp.concatenate([at_ref[0, rs, ls], rt_ref[0, rs, ls]], axis=0)
            z[ch] = _dot_nt(x2, st[ch].astype(BF16))
        for ch in chains:
            d, qd = ch
            c = j if d == 0 else nc - 1 - j
            rs, ls = slice(c * CHUNK, (c + 1) * CHUNK), slice(qd * QUAD, (qd + 1) * QUAD)
            (_, _, mrb_ref, uv_ref, yv_ref, bp_ref, kp_ref, pt_ref), v_ref, y_ref = dirs[d]
            u = z[ch][:CHUNK] + uv_ref[0, rs, ls]
            y_ref[0, rs, ls] = z[ch][CHUNK:] + _dot(mrb_ref[0, rs, ls], _bd16(u)) + yv_ref[0, rs, ls]
            uv2 = jnp.concatenate([u.astype(BF16), v_ref[0, rs, ls]], axis=0)
            bk2 = jnp.concatenate([bp_ref[0, rs, ls], kp_ref[0, rs, ls]], axis=0)
            st[ch] = st[ch] * pt_ref[0, c][:, ls] + jnp.where(diag, _dot_tn(uv2, bk2), 0.0)
    for ch in chains:
        st_ref[ch[0], ch[1]] = st[ch]


def _rwkv_scan(fwd_arrs, bwd_arrs, v16):
    b, s, _ = v16.shape
    nc = min(SCAN_CHUNKS, s // CHUNK)
    rows = nc * CHUNK
    steps = s // rows
    f_tok = pl.BlockSpec((1, rows, RWKV_DIM), lambda bi, i: (bi, i, 0))
    b_tok = pl.BlockSpec((1, rows, RWKV_DIM), lambda bi, i: (bi, steps - 1 - i, 0))
    f_pt = pl.BlockSpec((1, nc, 1, RWKV_DIM), lambda bi, i: (bi, i, 0, 0))
    b_pt = pl.BlockSpec((1, nc, 1, RWKV_DIM), lambda bi, i: (bi, steps - 1 - i, 0, 0))
    y_shape = jax.ShapeDtypeStruct((b, s, RWKV_DIM), F32)
    return pl.pallas_call(
        functools.partial(_rwkv_scan_body, nc=nc),
        grid=(b, steps),
        in_specs=[f_tok] * 7 + [f_pt] + [b_tok] * 7 + [b_pt] + [f_tok, b_tok],
        out_specs=[f_tok, b_tok],
        out_shape=[y_shape, y_shape],
        scratch_shapes=[pltpu.VMEM((2, RWKV_DIM // QUAD, QUAD, QUAD), F32)],
        compiler_params=_params("parallel", "arbitrary"),
        name="rwkv_scan",
    )(*fwd_arrs, *bwd_arrs, v16, v16)


def _rwkv_post_body(yf_ref, yb_ref, bonus_ref, gate_ref, lg_ref, lb_ref, ones_ref, o_ref):
    ones = ones_ref[...]
    wkv = yf_ref[...] + yb_ref[...]
    mean = _head_sum(wkv, ones, 2) * (1.0 / HEAD_DIM)
    dev = wkv - mean
    var = _head_sum(dev * dev, ones, 1) * (1.0 / HEAD_DIM)
    y = dev * lax.rsqrt(var + LNX_EPS) * lg_ref[...] + lb_ref[...] + bonus_ref[...]
    o_ref[...] = (y * gate_ref[...]).astype(BF16)


def _rwkv_post(yf, yb, bonus, gate, lnx_g, lnx_b, ones_bd):
    t, w = yf.shape
    tm = min(TOKEN_TILE, t)
    row = pl.BlockSpec((tm, w), lambda i: (i, 0))
    return pl.pallas_call(
        _rwkv_post_body,
        grid=(t // tm,),
        in_specs=[row, row, row, row, _resident((1, w)), _resident((1, w)), _resident(ones_bd.shape)],
        out_specs=row,
        out_shape=jax.ShapeDtypeStruct((t, w), BF16),
        compiler_params=_params("parallel"),
        name="rwkv_post",
    )(yf, yb, bonus, gate, lnx_g, lnx_b, ones_bd)


N_PAIRS = 4


def _pair_queries(q_ref, pair):
    qp = q_ref[0, :, pair * LANES:(pair + 1) * LANES]
    lane = lax.broadcasted_iota(jnp.int32, qp.shape, 1)
    zero = jnp.zeros_like(qp)
    return jnp.concatenate([jnp.where(lane < HEAD_DIM, qp, zero), jnp.where(lane >= HEAD_DIM, qp, zero)], axis=0)


def _store_pair(o_ref, pair, o, tq):
    lane = lax.broadcasted_iota(jnp.int32, (tq, LANES), 1)
    o_ref[0, :, pair * LANES:(pair + 1) * LANES] = jnp.where(lane < HEAD_DIM, o[:tq], o[tq:]).astype(BF16)


def _values_and_ones(vd):
    return jnp.concatenate([vd, jnp.ones_like(vd)], axis=1)


def _skewed_pairs(scores, softmax, finish):
    sc, pr = {}, {}
    for step in range(N_PAIRS + 2):
        if step < N_PAIRS:
            sc[step] = scores(step)
        if 1 <= step <= N_PAIRS:
            pr[step - 1] = softmax(step - 1, sc.pop(step - 1))
        if step >= 2:
            finish(step - 2, pr.pop(step - 2))


def _ga_body(q_ref, kv_ref, o_ref, *, tq):
    def scores(p):
        return _dot_nt(_pair_queries(q_ref, p), kv_ref[0, :, (p // 2) * LANES:(p // 2 + 1) * LANES])

    def softmax(p, sc):
        return jnp.exp(sc - jnp.max(sc, axis=-1, keepdims=True)).astype(BF16)

    def finish(p, e):
        ov = _dot(e, _values_and_ones(kv_ref[0, :, (2 + p // 2) * LANES:(3 + p // 2) * LANES]))
        _store_pair(o_ref, p, ov[:, :LANES] / ov[:, LANES:], tq)

    _skewed_pairs(scores, softmax, finish)


def _global_attention(q, kv):
    b, s, w = q.shape
    tq = min(GA_Q_TILE, s)
    qs = pl.BlockSpec((1, tq, w), lambda bi, i: (bi, i, 0))
    kvs = pl.BlockSpec((1, s, w), lambda bi, i: (bi, 0, 0))
    return pl.pallas_call(
        functools.partial(_ga_body, tq=tq),
        grid=(b, s // tq),
        in_specs=[qs, kvs],
        out_specs=qs,
        out_shape=jax.ShapeDtypeStruct((b, s, w), BF16),
        compiler_params=_params("parallel", "arbitrary"),
        name="global_attn",
    )(q, kv)


def _wa_window_start(i, tq, span, seq):
    return jnp.clip(i * tq - WINDOW, 0, seq - span)


def _wa_body(sink_ref, q_ref, kv_ref, band_ref, o_ref, *, tq, span, seq):
    start = pl.multiple_of(_wa_window_start(pl.program_id(1), tq, span, seq), WINDOW)
    valid = band_ref[0] > 0.5
    valid2 = jnp.concatenate([valid, valid], axis=0)

    def scores(p):
        kd = kv_ref[0, pl.ds(start, span), (p // 2) * LANES:(p // 2 + 1) * LANES]
        return jnp.where(valid2, _dot_nt(_pair_queries(q_ref, p), kd), NEG_INF)

    def softmax(p, sc):
        e, sink_e = [], []
        for h in range(2):
            sc_h = sc[h * tq:(h + 1) * tq]
            sink = sink_ref[2 * p + h]
            m = jnp.maximum(jnp.max(sc_h, axis=-1, keepdims=True), sink)
            e.append(jnp.exp(sc_h - m).astype(BF16))
            sink_e.append(jnp.exp(sink - m))
        return jnp.concatenate(e, axis=0), jnp.concatenate(sink_e, axis=0)

    def finish(p, es):
        e, sink_e = es
        ov = _dot(e, _values_and_ones(kv_ref[0, pl.ds(start, span), (2 + p // 2) * LANES:(3 + p // 2) * LANES]))
        _store_pair(o_ref, p, ov[:, :LANES] / (ov[:, LANES:] + sink_e), tq)

    _skewed_pairs(scores, softmax, finish)


def _window_attention(q, kv, sink):
    b, s, w = q.shape
    tq = min(WA_Q_TILE, s)
    span = min(tq + 2 * WINDOW, s)
    nblk = s // tq
    offsets = sorted({int(np.clip(i * tq - WINDOW, 0, s - span)) - i * tq for i in range(nblk)}, reverse=True)
    rel = jnp.arange(span)[None, :] - jnp.arange(tq)[:, None]
    band = jnp.stack([(jnp.abs(rel + off) <= WINDOW).astype(F32) for off in offsets])
    which = lambda i: jnp.where(i == 0, 0, jnp.where(i == nblk - 1, len(offsets) - 1, min(1, len(offsets) - 1)))
    qs = pl.BlockSpec((1, tq, w), lambda bi, i, *_: (bi, i, 0))
    kvs = pl.BlockSpec((1, s, w), lambda bi, i, *_: (bi, 0, 0))
    bands = pl.BlockSpec((1, tq, span), lambda bi, i, *_: (which(i), 0, 0))
    return pl.pallas_call(
        functools.partial(_wa_body, tq=tq, span=span, seq=s),
        grid_spec=pltpu.PrefetchScalarGridSpec(
            num_scalar_prefetch=1, grid=(b, nblk), in_specs=[qs, kvs, bands], out_specs=qs),
        out_shape=jax.ShapeDtypeStruct((b, s, w), BF16),
        compiler_params=_params("parallel", "arbitrary"),
        name="window_attn",
    )(sink, q, kv, band)


def _merge_body(x_ref, g_ref, wg_ref, ya_ref, yb_ref, yc_ref, woa_ref, wob_ref, woc_ref, wout_ref, o_ref):
    x = x_ref[...]
    h = _rmsnorm(x, g_ref[...]).astype(BF16)
    merged = None
    for j, (y_ref, wo_ref) in enumerate(((ya_ref, woa_ref), (yb_ref, wob_ref), (yc_ref, woc_ref))):
        gate = jax.nn.sigmoid(_dot(h, wg_ref[:, j * D_MODEL:(j + 1) * D_MODEL]))
        term = gate * _dot(y_ref[...], wo_ref[...])
        merged = term if merged is None else merged + term
    o_ref[...] = x + _dot(merged.astype(BF16), wout_ref[...])


def _merge(x, g, wg, ya, yb, yc, woa, wob, woc, wout):
    t, d = x.shape
    tm = min(TOKEN_TILE, t)
    row = lambda w_: pl.BlockSpec((tm, w_), lambda i: (i, 0))
    return pl.pallas_call(
        _merge_body,
        grid=(t // tm,),
        in_specs=[row(d), _resident((1, d)), _resident(wg.shape), row(512), row(512), row(512),
                  _resident(woa.shape), _resident(wob.shape), _resident(woc.shape), _resident(wout.shape)],
        out_specs=row(d),
        out_shape=jax.ShapeDtypeStruct((t, d), F32),
        compiler_params=_params("parallel"),
        name="merge",
    )(x, g, wg, ya, yb, yc, woa, wob, woc, wout)


def _rope_tables(seq):
    t = jnp.arange(seq)

    def angles(pos, dim):
        inv_freq = ROPE_THETA ** (-jnp.arange(0, dim, 2, dtype=F32) / dim)
        return pos.astype(F32)[:, None] * inv_freq[None, :]

    ang_row, ang_col = angles(t // GRID_W, HEAD_DIM // 2), angles(t % GRID_W, HEAD_DIM // 2)
    ang_1d = angles(t, HEAD_DIM)
    ga_cos = jnp.concatenate([jnp.cos(ang_row)] * 2 + [jnp.cos(ang_col)] * 2, axis=1)
    ga_sin = jnp.concatenate([-jnp.sin(ang_row), jnp.sin(ang_row), -jnp.sin(ang_col), jnp.sin(ang_col)], axis=1)
    wa_cos = jnp.concatenate([jnp.cos(ang_1d)] * 2, axis=1)
    wa_sin = jnp.concatenate([-jnp.sin(ang_1d), jnp.sin(ang_1d)], axis=1)
    return tuple(jnp.concatenate([a, a], axis=1) for a in (ga_cos, ga_sin, wa_cos, wa_sin))


def _two_way(w):
    z = jnp.zeros_like(w[0])
    return jnp.concatenate([jnp.concatenate([w[0], z], axis=1), jnp.concatenate([z, w[1]], axis=1)], axis=0)


def kernel(x, ffn1_norm, ffn1_w_up, ffn1_w_down, mix_norm, w_in, rwkv_mu, rwkv_w0, rwkv_w2, rwkv_a0, rwkv_a2, rwkv_g2, rwkv_k_k, rwkv_k_a, rwkv_r_k, rwkv_lnx_g, rwkv_lnx_b, ga_q_norm, ga_k_norm, wa_sink, w_o_rwkv, w_o_ga, w_o_wa, w_out, ffn2_norm, ffn2_w_up, ffn2_w_down, final_norm):
    bsz, seq, dm = x.shape
    depth = w_in.shape[0]
    tokens = bsz * seq
    tabs = _rope_tables(seq)
    idx = jnp.arange(512)
    ones_bd = (idx[:, None] // HEAD_DIM == idx[None, :] // HEAD_DIM).astype(BF16)
    row1 = lambda a: a.reshape(1, -1)
    final_g = row1(final_norm)

    xt = x.reshape(tokens, dm)
    for l in range(depth):
        xt = _ffn(xt, row1(ffn1_norm[l]), ffn1_w_up[l].astype(BF16), ffn1_w_down[l].astype(BF16), final_g, False)

        w_mix = w_in[l, :, :MIX_COLS].astype(BF16)
        w_gate = w_in[l, :, MIX_COLS:].astype(BF16)
        p_rwkv, ga_q, ga_kv, wa_q, wa_kv = _inproj(
            xt, row1(mix_norm[l]), w_mix, ones_bd, row1(jnp.tile(ga_q_norm[l], 8)), row1(jnp.tile(ga_k_norm[l], 8)),
            tabs, seq)

        outs = _rwkv_prep(
            p_rwkv.reshape(bsz, seq, RWKV_COLS), row1(rwkv_mu[l]), row1(rwkv_w0[l]), _two_way(rwkv_w2[l]).astype(BF16),
            row1(rwkv_a0[l]), _two_way(rwkv_a2[l]).astype(BF16), rwkv_g2[l].astype(BF16), row1(rwkv_k_k[l]),
            row1(rwkv_k_a[l]), row1(rwkv_r_k[l]), ones_bd)
        v16, bonus, gate = outs[16:19]
        y_f, y_b = _rwkv_scan(outs[0:8], outs[8:16], v16)
        flat = lambda a: a.reshape(tokens, RWKV_DIM)
        y_a = _rwkv_post(flat(y_f), flat(y_b), flat(bonus), flat(gate), row1(rwkv_lnx_g[l]), row1(rwkv_lnx_b[l]), ones_bd)

        y_g = _global_attention(ga_q.reshape(bsz, seq, 512), ga_kv.reshape(bsz, seq, 512))
        y_w = _window_attention(wa_q.reshape(bsz, seq, 512), wa_kv.reshape(bsz, seq, 512), wa_sink[l])

        xt = _merge(xt, row1(mix_norm[l]), w_gate, y_a, flat(y_g), flat(y_w), w_o_rwkv[l].astype(BF16),
                    w_o_ga[l].astype(BF16), w_o_wa[l].astype(BF16), w_out[l].astype(BF16))

        xt = _ffn(xt, row1(ffn2_norm[l]), ffn2_w_up[l].astype(BF16), ffn2_w_down[l].astype(BF16), final_g,
                  l == depth - 1)
    return xt.reshape(bsz, seq, dm)
```

```python
import functools

import jax
import jax.numpy as jnp
import numpy as np
from jax import lax
from jax.experimental import pallas as pl
from jax.experimental.pallas import tpu as pltpu

F32 = jnp.float32
BF16 = jnp.bfloat16

D_MODEL = 1024
HEAD_DIM = 64
RWKV_DIM = 512
RWKV_COLS = 1920
GA_COLS = 768
WA_COLS = 768
MIX_COLS = RWKV_COLS + GA_COLS + WA_COLS
D_FF = 2816
GRID_W = 64
WINDOW = 128
ROPE_THETA = 10000.0
NORM_EPS = 1e-6
LNX_EPS = HEAD_DIM * 1e-5
NEG_INF = -1e30
ATTN_SCALE = HEAD_DIM ** -0.5

CHUNK = 64
QUAD = 4 * HEAD_DIM
LANES = 128
VMEM_LIMIT = 56 * 1024 * 1024

TOKEN_TILE = 512
FF_CHUNK = 1408
PREP_TILE = 512
PREP_SUB = 128
SCAN_CHUNKS = 4
SCAN_BATCH = 2
GA_Q_TILE = 256
WA_Q_TILE = 256


def _params(*sem):
    return pltpu.CompilerParams(dimension_semantics=sem, vmem_limit_bytes=VMEM_LIMIT)


def _resident(shape):
    nd = len(shape)
    return pl.BlockSpec(shape, lambda *_: (0,) * nd, pipeline_mode=pl.Buffered(1))


def _rmsnorm(x, g):
    return x * lax.rsqrt(jnp.mean(x * x, axis=-1, keepdims=True) + NORM_EPS) * g


def _row_halves(rows):
    if rows % 32:
        return [slice(0, rows)]
    return [slice(0, rows // 2), slice(rows // 2, rows)]


def _dot(a, b):
    return jnp.dot(a, b, preferred_element_type=F32)


def _dot_nt(a, b):
    return lax.dot_general(a, b, (((1,), (1,)), ((), ())), preferred_element_type=F32)


def _dot_tn(a, b):
    return lax.dot_general(a, b, (((0,), (0,)), ((), ())), preferred_element_type=F32)


def _split_bf16(x, terms):
    out = []
    for _ in range(terms - 1):
        hi = x.astype(BF16)
        out.append(hi)
        x = x - hi.astype(F32)
    out.append(x.astype(BF16))
    return out


def _head_sum(x, ones_bd, terms):
    w = x.shape[-1]
    blk = min(w, QUAD)
    ones = ones_bd[:blk, :blk]
    parts = []
    for c0 in range(0, w, blk):
        acc = None
        for t in _split_bf16(x[:, c0:c0 + blk], terms):
            y = _dot(t, ones)
            acc = y if acc is None else acc + y
        parts.append(acc)
    return parts[0] if len(parts) == 1 else jnp.concatenate(parts, axis=1)


def _ffn_body(x_ref, g_ref, wup_ref, wdn_ref, gf_ref, o_ref, act_ref, *, final_norm):
    halves = _row_halves(x_ref.shape[0])
    h = [_rmsnorm(x_ref[rs, :], g_ref[...]).astype(BF16) for rs in halves]
    for c in range(D_FF // FF_CHUNK):
        lo, hi = c * FF_CHUNK, (c + 1) * FF_CHUNK
        for rs, hh in zip(halves, h):
            gate = _dot(hh, wup_ref[:, lo:hi])
            up = _dot(hh, wup_ref[:, D_FF + lo:D_FF + hi])
            act_ref[rs, lo:hi] = (gate * jax.nn.sigmoid(gate) * up).astype(BF16)
    for rs in halves:
        y = x_ref[rs, :] + 0.5 * _dot(act_ref[rs, :], wdn_ref[...])
        if final_norm:
            y = _rmsnorm(y, gf_ref[...])
        o_ref[rs, :] = y


def _ffn(x, g, w_up, w_down, g_final, final_norm):
    t, d = x.shape
    tm = min(TOKEN_TILE, t)
    row = pl.BlockSpec((tm, d), lambda i: (i, 0))
    return pl.pallas_call(
        functools.partial(_ffn_body, final_norm=final_norm),
        grid=(t // tm,),
        in_specs=[row, _resident((1, d)), _resident(w_up.shape), _resident(w_down.shape), _resident((1, d))],
        out_specs=row,
        out_shape=jax.ShapeDtypeStruct((t, d), F32),
        scratch_shapes=[pltpu.VMEM((tm, D_FF), BF16)],
        compiler_params=_params("parallel"),
        name="ffn_final" if final_norm else "ffn",
    )(x, g, w_up, w_down, g_final)


def _rope(t, cos, sin_signed, half):
    w = t.shape[-1]
    lane = lax.broadcasted_iota(jnp.int32, t.shape, 1)
    first = (lane % (2 * half)) < half
    partner = jnp.where(first, pltpu.roll(t, w - half, 1), pltpu.roll(t, half, 1))
    return t * cos + partner * sin_signed


def _dup_heads(t):
    lane = lax.broadcasted_iota(jnp.int32, t.shape, 1)
    swapped = pltpu.roll(t, HEAD_DIM, 1)
    low = lane < HEAD_DIM
    return jnp.where(low, t, swapped), jnp.where(low, swapped, t)


def _inproj_body(x_ref, g_ref, w_ref, ones_ref, gq_ref, gk_ref, cga_ref, sga_ref, cwa_ref, swa_ref,
                 prw_ref, gaq_ref, gakv_ref, waq_ref, wakv_ref):
    ones = ones_ref[...]

    def head_norm(t, gain):
        ms = _head_sum(t * t, ones, 1) * (1.0 / HEAD_DIM)
        return t * lax.rsqrt(ms + NORM_EPS) * gain

    tile4 = lambda tab: jnp.concatenate([tab] * 4, axis=1)

    def project(rs):
        h = _rmsnorm(x_ref[rs, :], g_ref[...]).astype(BF16)
        prw_ref[rs, :] = _dot(h, w_ref[:, :RWKV_COLS])
        return _dot(h, w_ref[:, RWKV_COLS:RWKV_COLS + GA_COLS]), _dot(h, w_ref[:, RWKV_COLS + GA_COLS:MIX_COLS])

    def finish(rs, ga, wa):
        cga, sga = cga_ref[rs, :], sga_ref[rs, :]
        q = _rope(head_norm(ga[:, :512], gq_ref[...]), tile4(cga), tile4(sga), 16)
        k = _rope(head_norm(ga[:, 512:640], gk_ref[:, :LANES]), cga, sga, 16)
        gaq_ref[rs, :] = (q * ATTN_SCALE).astype(BF16)
        k0, k1 = _dup_heads(k)
        v0, v1 = _dup_heads(ga[:, 640:768])
        gakv_ref[rs, :] = jnp.concatenate([k0, k1, v0, v1], axis=1).astype(BF16)
        cwa, swa = cwa_ref[rs, :], swa_ref[rs, :]
        q = _rope(wa[:, :512], tile4(cwa), tile4(swa), 32)
        k = _rope(wa[:, 512:640], cwa, swa, 32)
        waq_ref[rs, :] = (q * ATTN_SCALE).astype(BF16)
        k0, k1 = _dup_heads(k)
        v0, v1 = _dup_heads(wa[:, 640:768])
        wakv_ref[rs, :] = jnp.concatenate([k0, k1, v0, v1], axis=1).astype(BF16)

    halves = _row_halves(x_ref.shape[0])
    projected = [project(rs) for rs in halves]
    for rs, (ga, wa) in zip(halves, projected):
        finish(rs, ga, wa)


def _inproj(x, g, w, ones_bd, gq, gk, tabs, seq):
    t, d = x.shape
    tm = min(TOKEN_TILE, seq)
    per_seq = seq // tm
    row = lambda w_: pl.BlockSpec((tm, w_), lambda i: (i, 0))
    tab = pl.BlockSpec((tm, LANES), lambda i: (i % per_seq, 0))
    return pl.pallas_call(
        _inproj_body,
        grid=(t // tm,),
        in_specs=[row(d), _resident((1, d)), _resident(w.shape), _resident(ones_bd.shape),
                  _resident(gq.shape), _resident(gk.shape), tab, tab, tab, tab],
        out_specs=[row(RWKV_COLS), row(512), row(512), row(512), row(512)],
        out_shape=[jax.ShapeDtypeStruct((t, RWKV_COLS), F32)] + [jax.ShapeDtypeStruct((t, 512), BF16)] * 4,
        compiler_params=_params("parallel"),
        name="inproj",
    )(x, g, w, ones_bd, gq, gk, *tabs)


def _block_diag4(x):
    head = lax.broadcasted_iota(jnp.int32, x.shape, 1) // HEAD_DIM
    return jnp.concatenate([jnp.where(head == h, x, jnp.zeros_like(x)) for h in range(4)], axis=0)


def _bd16(x):
    return _block_diag4(x).astype(BF16)


def _rwkv_prep_body(p_ref, hp_ref, hn_ref, mu_ref, w0_ref, w2_ref, a0_ref, a2_ref, g2_ref, kk_ref, ka_ref, rk_ref,
                    ones_ref, *out_refs, ts):
    i = pl.program_id(1)
    n = pl.num_programs(1)
    edge_prev = jnp.where(i > 0, hp_ref[0, 7:8, :], 0.0)
    edge_next = jnp.where(i < n - 1, hn_ref[0, 0:1, :], 0.0)
    consts = (mu_ref, w0_ref, w2_ref, a0_ref, a2_ref, g2_ref, kk_ref, ka_ref, rk_ref, ones_ref)
    chain_stage = None
    for r0 in range(0, ts, PREP_SUB):
        prev_row = edge_prev if r0 == 0 else p_ref[0, r0 - 1:r0, :]
        next_row = edge_next if r0 + PREP_SUB == ts else p_ref[0, r0 + PREP_SUB:r0 + PREP_SUB + 1, :]
        state = {}
        vector_stage = _prep_vector_stage(p_ref, prev_row, next_row, consts, out_refs, r0, PREP_SUB, state)
        _alternate(chain_stage, vector_stage)
        chain_stage = _prep_chain_stage(state, out_refs, r0, PREP_SUB)
    _alternate(chain_stage, None)


def _alternate(first, second):
    live = [g for g in (first, second) if g is not None]
    while live:
        for g in list(live):
            try:
                next(g)
            except StopIteration:
                live.remove(g)


def _prep_vector_stage(p_ref, prev_row, next_row, consts, out_refs, r0, ts, state):
    mu_ref, w0_ref, w2_ref, a0_ref, a2_ref, g2_ref, kk_ref, ka_ref, rk_ref, ones_ref = consts
    dir_refs = (out_refs[0:8], out_refs[8:16])
    v_ref, bonus_ref, gate_ref = out_refs[16:19]
    rows = slice(r0, r0 + ts)

    def shifted_mix(lo, hi):
        p = p_ref[0, rows, lo:hi]
        row = lax.broadcasted_iota(jnp.int32, p.shape, 0)
        prev = jnp.where(row == 0, prev_row[:, lo:hi], pltpu.roll(p, 1, 0))
        nxt = jnp.where(row == ts - 1, next_row[:, lo:hi], pltpu.roll(p, ts - 1, 0))
        return p + mu_ref[:, lo:hi] * (0.5 * (prev + nxt) - p)

    lora = shifted_mix(1536, 1920)
    wl, al, gl = lora[:, 0:128], lora[:, 128:256], lora[:, 256:384]
    yield
    r = shifted_mix(0, 512)
    yield
    k = shifted_mix(512, 1024)
    yield
    v = shifted_mix(1024, 1536)
    yield
    ones = ones_ref[...]

    u = w0_ref[...] + _dot(jnp.tanh(wl).astype(BF16), w2_ref[...])
    logw = (-float(np.exp(-0.5))) * jax.nn.sigmoid(u)
    asig = jax.nn.sigmoid(a0_ref[...] + _dot(al.astype(BF16), a2_ref[...]))
    yield
    kkr = k * kk_ref[...]
    kk = kkr * lax.rsqrt(jnp.maximum(_head_sum(kkr * kkr, ones, 1), 1e-24))
    k_a = ka_ref[...]
    kd = [k * (1.0 + (asig[:, d * 512:(d + 1) * 512] - 1.0) * k_a) for d in range(2)]
    yield
    bonus_ref[0, rows] = (_head_sum(r * rk_ref[...] * (kd[0] + kd[1]), ones, 1) * v).astype(BF16)
    gate_ref[0, rows] = _dot(jax.nn.sigmoid(gl).astype(BF16), g2_ref[...]).astype(BF16)
    v_ref[0, rows] = v.astype(BF16)
    yield

    rr = lax.broadcasted_iota(jnp.int32, (ts, ts), 0)
    cc = lax.broadcasted_iota(jnp.int32, (ts, ts), 1)
    same = (rr // CHUNK) == (cc // CHUNK)

    scaled = []
    for d in range(2):
        bp_ref, kp_ref, pt_ref = dir_refs[d][5:8]
        rt_ref = dir_refs[d][1]
        before = (cc <= rr) if d == 0 else (cc >= rr)
        lhs = jnp.where(same & before, 1.0, 0.0).astype(BF16)
        lw = logw[:, d * 512:(d + 1) * 512]
        cl = None
        for term in _split_bf16(lw, 2):
            y = _dot(lhs, term)
            cl = y if cl is None else cl + y
        yield
        ends = [c * CHUNK + (CHUNK - 1 if d == 0 else 0) for c in range(ts // CHUNK)]
        p_tot = [jnp.exp(cl[e:e + 1, :]) for e in ends]
        for c in range(ts // CHUNK):
            pt_ref[0, r0 // CHUNK + c] = p_tot[c]
        e_neg = jnp.exp(-cl)
        b_raw = kk * asig[:, d * 512:(d + 1) * 512]
        r_t = r * jnp.exp(cl)
        rt_ref[0, rows] = r_t.astype(BF16)
        yield
        e_out = e_neg * jnp.concatenate([jnp.broadcast_to(pc, (CHUNK, RWKV_DIM)) for pc in p_tot], axis=0)
        bp_ref[0, rows] = (b_raw * e_out).astype(BF16)
        kp_ref[0, rows] = (kd[d] * e_out).astype(BF16)
        yield
        scaled.append((-kk * jnp.exp(cl - lw), b_raw * e_neg, kd[d] * e_neg, r_t))
        yield
    state.update(scaled=scaled, v=v)


def _prep_chain_stage(state, out_refs, r0, ts):
    scaled, v = state["scaled"], state["v"]
    dir_refs = (out_refs[0:8], out_refs[8:16])
    tt = lax.broadcasted_iota(jnp.int32, (CHUNK, QUAD), 0)
    ss = lax.broadcasted_iota(jnp.int32, (CHUNK, QUAD), 1) % CHUNK
    eye = jnp.where(tt == ss, 1.0, 0.0)
    chains = [(d, c, qd) for d in range(2) for c in range(ts // CHUNK) for qd in range(RWKV_DIM // QUAD)]
    cut = lambda arr, c, qd: arr[c * CHUNK:(c + 1) * CHUNK, qd * QUAD:(qd + 1) * QUAD]
    strict = [ss < tt, ss > tt]
    incl = [ss <= tt, ss >= tt]
    at, x2 = {}, {}
    for ch in chains:
        d, c, qd = ch
        at[ch] = cut(scaled[d][0], c, qd)
        x2[ch] = jnp.concatenate([at[ch], cut(scaled[d][3], c, qd)], axis=0).astype(BF16)
    w_b = {ch: _dot_nt(x2[ch], _bd16(cut(scaled[ch[0]][1], ch[1], ch[2]))) for ch in chains}
    yield
    w_k = {ch: _dot_nt(x2[ch], _bd16(cut(scaled[ch[0]][2], ch[1], ch[2]))) for ch in chains}
    yield
    a_ab, gy = {}, {}
    for ch in chains:
        d, c, qd = ch
        a_ab[ch] = jnp.where(strict[d], w_b[ch][:CHUNK], 0.0)
        dir_refs[d][2][0, r0 + c * CHUNK:r0 + (c + 1) * CHUNK, qd * QUAD:(qd + 1) * QUAD] = (
            jnp.where(incl[d], w_b[ch][CHUNK:], 0.0).astype(BF16))
        lk = jnp.concatenate([jnp.where(strict[d], w_k[ch][:CHUNK], 0.0),
                              jnp.where(incl[d], w_k[ch][CHUNK:], 0.0)], axis=0)
        gy[ch] = _dot(lk.astype(BF16), _bd16(cut(v, c, qd)))
    yield
    tm = {ch: eye + a_ab[ch] for ch in chains}
    pw = {ch: _dot(a_ab[ch].astype(BF16), _bd16(a_ab[ch])) for ch in chains}
    yield
    for _ in range(4):
        both = {ch: _dot(jnp.concatenate([pw[ch], tm[ch]], axis=0).astype(BF16), _bd16(pw[ch])) for ch in chains}
        tm = {ch: tm[ch] + both[ch][CHUNK:] for ch in chains}
        pw = {ch: both[ch][:CHUNK] for ch in chains}
        yield
    tm = {ch: (tm[ch] + _dot(tm[ch].astype(BF16), _bd16(pw[ch]))).astype(BF16) for ch in chains}
    yield
    for ch in chains:
        d, c, qd = ch
        rs, ls = slice(r0 + c * CHUNK, r0 + (c + 1) * CHUNK), slice(qd * QUAD, (qd + 1) * QUAD)
        at_ref, _, _, uv_ref, yv_ref = dir_refs[d][0:5]
        uv_ref[0, rs, ls] = _dot(tm[ch], _bd16(gy[ch][:CHUNK])).astype(BF16)
        at_ref[0, rs, ls] = _dot(tm[ch], _bd16(at[ch])).astype(BF16)
        yv_ref[0, rs, ls] = gy[ch][CHUNK:].astype(BF16)


def _rwkv_prep(p, mu, w0, w2bd, a0, a2bd, g2, k_k, k_a, r_k, ones_bd):
    b, s, _ = p.shape
    ts = min(PREP_TILE, s)
    nt = s // ts
    h8 = ts // 8
    main = pl.BlockSpec((1, ts, RWKV_COLS), lambda bi, i: (bi, i, 0))
    halo_prev = pl.BlockSpec((1, 8, RWKV_COLS), lambda bi, i: (bi, jnp.maximum(i * h8 - 1, 0), 0))
    halo_next = pl.BlockSpec((1, 8, RWKV_COLS), lambda bi, i: (bi, jnp.minimum((i + 1) * h8, s // 8 - 1), 0))
    tok = pl.BlockSpec((1, ts, RWKV_DIM), lambda bi, i: (bi, i, 0))
    ptot = pl.BlockSpec((1, ts // CHUNK, 1, RWKV_DIM), lambda bi, i: (bi, i, 0, 0))
    tok16 = jax.ShapeDtypeStruct((b, s, RWKV_DIM), BF16)
    tok32 = jax.ShapeDtypeStruct((b, s, RWKV_DIM), F32)
    pt32 = jax.ShapeDtypeStruct((b, s // CHUNK, 1, RWKV_DIM), F32)
    per_dir_specs = [tok, tok, tok, tok, tok, tok, tok, ptot]
    per_dir_shapes = [tok16] * 7 + [pt32]
    consts = [mu, w0, w2bd, a0, a2bd, g2, k_k, k_a, r_k, ones_bd]
    return pl.pallas_call(
        functools.partial(_rwkv_prep_body, ts=ts),
        grid=(b, nt),
        in_specs=[main, halo_prev, halo_next] + [_resident(c.shape) for c in consts],
        out_specs=per_dir_specs * 2 + [tok, tok, tok],
        out_shape=per_dir_shapes * 2 + [tok16, tok16, tok16],
        compiler_params=_params("parallel", "parallel"),
        name="rwkv_prep",
    )(p, p, p, *consts)


def _rwkv_scan_body(*refs, nc, nb):
    fwd, bwd = refs[0:8], refs[8:16]
    vf_ref, vb_ref = refs[16:18]
    yf_ref, yb_ref = refs[18:20]
    st_ref = refs[20]
    step = pl.program_id(1)

    @pl.when(step == 0)
    def _():
        st_ref[...] = jnp.zeros_like(st_ref)

    rr = lax.broadcasted_iota(jnp.int32, (QUAD, QUAD), 0) // HEAD_DIM
    cc = lax.broadcasted_iota(jnp.int32, (QUAD, QUAD), 1) // HEAD_DIM
    diag = rr == cc
    dirs = ((fwd, vf_ref, yf_ref), (bwd, vb_ref, yb_ref))
    chains = [(bi, d, qd) for bi in range(nb) for d in range(2) for qd in range(RWKV_DIM // QUAD)]
    st = {ch: st_ref[ch] for ch in chains}
    for j in range(nc):
        z = {}
        for ch in chains:
            bi, d, qd = ch
            c = j if d == 0 else nc - 1 - j
            rs, ls = slice(c * CHUNK, (c + 1) * CHUNK), slice(qd * QUAD, (qd + 1) * QUAD)
            at_ref, rt_ref = dirs[d][0][0:2]
            x2 = jnp.concatenate([at_ref[bi, rs, ls], rt_ref[bi, rs, ls]], axis=0)
            z[ch] = _dot_nt(x2, st[ch].astype(BF16))
        for ch in chains:
            bi, d, qd = ch
            c = j if d == 0 else nc - 1 - j
            rs, ls = slice(c * CHUNK, (c + 1) * CHUNK), slice(qd * QUAD, (qd + 1) * QUAD)
            (_, _, mrb_ref, uv_ref, yv_ref, bp_ref, kp_ref, pt_ref), v_ref, y_ref = dirs[d]
            u = z[ch][:CHUNK] + uv_ref[bi, rs, ls]
            y = z[ch][CHUNK:] + _dot(mrb_ref[bi, rs, ls], _bd16(u)) + yv_ref[bi, rs, ls]
            y_ref[bi, rs, ls] = y.astype(BF16)
            uv2 = jnp.concatenate([u.astype(BF16), v_ref[bi, rs, ls]], axis=0)
            bk2 = jnp.concatenate([bp_ref[bi, rs, ls], kp_ref[bi, rs, ls]], axis=0)
            st[ch] = st[ch] * pt_ref[bi, c][:, ls] + jnp.where(diag, _dot_tn(uv2, bk2), 0.0)
    for ch in chains:
        st_ref[ch] = st[ch]


def _rwkv_scan(fwd_arrs, bwd_arrs, v16):
    b, s, _ = v16.shape
    nc = min(SCAN_CHUNKS, s // CHUNK)
    nb = SCAN_BATCH if b % SCAN_BATCH == 0 else 1
    rows = nc * CHUNK
    steps = s // rows
    f_tok = pl.BlockSpec((nb, rows, RWKV_DIM), lambda bi, i: (bi, i, 0))
    b_tok = pl.BlockSpec((nb, rows, RWKV_DIM), lambda bi, i: (bi, steps - 1 - i, 0))
    f_pt = pl.BlockSpec((nb, nc, 1, RWKV_DIM), lambda bi, i: (bi, i, 0, 0))
    b_pt = pl.BlockSpec((nb, nc, 1, RWKV_DIM), lambda bi, i: (bi, steps - 1 - i, 0, 0))
    y_shape = jax.ShapeDtypeStruct((b, s, RWKV_DIM), BF16)
    return pl.pallas_call(
        functools.partial(_rwkv_scan_body, nc=nc, nb=nb),
        grid=(b // nb, steps),
        in_specs=[f_tok] * 7 + [f_pt] + [b_tok] * 7 + [b_pt] + [f_tok, b_tok],
        out_specs=[f_tok, b_tok],
        out_shape=[y_shape, y_shape],
        scratch_shapes=[pltpu.VMEM((nb, 2, RWKV_DIM // QUAD, QUAD, QUAD), F32)],
        compiler_params=_params("parallel", "arbitrary"),
        name="rwkv_scan",
    )(*fwd_arrs, *bwd_arrs, v16, v16)


N_PAIRS = 4


def _pair_queries(q_ref, pair):
    qp = q_ref[0, :, pair * LANES:(pair + 1) * LANES]
    lane = lax.broadcasted_iota(jnp.int32, qp.shape, 1)
    zero = jnp.zeros_like(qp)
    return jnp.concatenate([jnp.where(lane < HEAD_DIM, qp, zero), jnp.where(lane >= HEAD_DIM, qp, zero)], axis=0)


def _store_pair(o_ref, pair, o, tq):
    lane = lax.broadcasted_iota(jnp.int32, (tq, LANES), 1)
    o_ref[0, :, pair * LANES:(pair + 1) * LANES] = jnp.where(lane < HEAD_DIM, o[:tq], o[tq:]).astype(BF16)


def _values_and_ones(vd):
    return jnp.concatenate([vd, jnp.ones_like(vd)], axis=1)


def _skewed_pairs(scores, softmax, finish):
    sc, pr = {}, {}
    for step in range(N_PAIRS + 2):
        if step < N_PAIRS:
            sc[step] = scores(step)
        if 1 <= step <= N_PAIRS:
            pr[step - 1] = softmax(step - 1, sc.pop(step - 1))
        if step >= 2:
            finish(step - 2, pr.pop(step - 2))


def _ga_body(q_ref, kv_ref, o_ref, *, tq):
    def scores(p):
        return _dot_nt(_pair_queries(q_ref, p), kv_ref[0, :, (p // 2) * LANES:(p // 2 + 1) * LANES])

    def softmax(p, sc):
        return jnp.exp(sc - jnp.max(sc, axis=-1, keepdims=True)).astype(BF16)

    def finish(p, e):
        ov = _dot(e, _values_and_ones(kv_ref[0, :, (2 + p // 2) * LANES:(3 + p // 2) * LANES]))
        _store_pair(o_ref, p, ov[:, :LANES] / ov[:, LANES:], tq)

    _skewed_pairs(scores, softmax, finish)


def _global_attention(q, kv):
    b, s, w = q.shape
    tq = min(GA_Q_TILE, s)
    qs = pl.BlockSpec((1, tq, w), lambda bi, i: (bi, i, 0))
    kvs = pl.BlockSpec((1, s, w), lambda bi, i: (bi, 0, 0))
    return pl.pallas_call(
        functools.partial(_ga_body, tq=tq),
        grid=(b, s // tq),
        in_specs=[qs, kvs],
        out_specs=qs,
        out_shape=jax.ShapeDtypeStruct((b, s, w), BF16),
        compiler_params=_params("parallel", "arbitrary"),
        name="global_attn",
    )(q, kv)


def _wa_window_start(i, tq, span, seq):
    return jnp.clip(i * tq - WINDOW, 0, seq - span)


def _wa_body(sink_ref, q_ref, kv_ref, band_ref, o_ref, *, tq, span, seq):
    start = pl.multiple_of(_wa_window_start(pl.program_id(1), tq, span, seq), WINDOW)
    valid = band_ref[0] > 0.5
    valid2 = jnp.concatenate([valid, valid], axis=0)

    def scores(p):
        kd = kv_ref[0, pl.ds(start, span), (p // 2) * LANES:(p // 2 + 1) * LANES]
        return jnp.where(valid2, _dot_nt(_pair_queries(q_ref, p), kd), NEG_INF)

    def softmax(p, sc):
        e, sink_e = [], []
        for h in range(2):
            sc_h = sc[h * tq:(h + 1) * tq]
            sink = sink_ref[2 * p + h]
            m = jnp.maximum(jnp.max(sc_h, axis=-1, keepdims=True), sink)
            e.append(jnp.exp(sc_h - m).astype(BF16))
            sink_e.append(jnp.exp(sink - m))
        return jnp.concatenate(e, axis=0), jnp.concatenate(sink_e, axis=0)

    def finish(p, es):
        e, sink_e = es
        ov = _dot(e, _values_and_ones(kv_ref[0, pl.ds(start, span), (2 + p // 2) * LANES:(3 + p // 2) * LANES]))
        _store_pair(o_ref, p, ov[:, :LANES] / (ov[:, LANES:] + sink_e), tq)

    _skewed_pairs(scores, softmax, finish)


def _window_attention(q, kv, sink):
    b, s, w = q.shape
    tq = min(WA_Q_TILE, s)
    span = min(tq + 2 * WINDOW, s)
    nblk = s // tq
    offsets = sorted({int(np.clip(i * tq - WINDOW, 0, s - span)) - i * tq for i in range(nblk)}, reverse=True)
    rel = jnp.arange(span)[None, :] - jnp.arange(tq)[:, None]
    band = jnp.stack([(jnp.abs(rel + off) <= WINDOW).astype(F32) for off in offsets])
    which = lambda i: jnp.where(i == 0, 0, jnp.where(i == nblk - 1, len(offsets) - 1, min(1, len(offsets) - 1)))
    qs = pl.BlockSpec((1, tq, w), lambda bi, i, *_: (bi, i, 0))
    kvs = pl.BlockSpec((1, s, w), lambda bi, i, *_: (bi, 0, 0))
    bands = pl.BlockSpec((1, tq, span), lambda bi, i, *_: (which(i), 0, 0))
    return pl.pallas_call(
        functools.partial(_wa_body, tq=tq, span=span, seq=s),
        grid_spec=pltpu.PrefetchScalarGridSpec(
            num_scalar_prefetch=1, grid=(b, nblk), in_specs=[qs, kvs, bands], out_specs=qs),
        out_shape=jax.ShapeDtypeStruct((b, s, w), BF16),
        compiler_params=_params("parallel", "arbitrary"),
        name="window_attn",
    )(sink, q, kv, band)


def _merge_body(x_ref, g_ref, wg_ref, wf_ref, wb_ref, bonus_ref, rgate_ref, lg_ref, lb_ref, ones_ref,
                yb_ref, yc_ref, woa_ref, wob_ref, woc_ref, wout_ref, o_ref):
    x = x_ref[...]
    h = _rmsnorm(x, g_ref[...]).astype(BF16)
    ones = ones_ref[...]
    wkv = wf_ref[...].astype(F32) + wb_ref[...].astype(F32)
    dev = wkv - _head_sum(wkv, ones, 2) * (1.0 / HEAD_DIM)
    var = _head_sum(dev * dev, ones, 1) * (1.0 / HEAD_DIM)
    y_a = dev * lax.rsqrt(var + LNX_EPS) * lg_ref[...] + lb_ref[...] + bonus_ref[...].astype(F32)
    y_a = (y_a * rgate_ref[...].astype(F32)).astype(BF16)
    merged = None
    for j, (y, wo_ref) in enumerate(((y_a, woa_ref), (yb_ref[...], wob_ref), (yc_ref[...], woc_ref))):
        gate = jax.nn.sigmoid(_dot(h, wg_ref[:, j * D_MODEL:(j + 1) * D_MODEL]))
        term = gate * _dot(y, wo_ref[...])
        merged = term if merged is None else merged + term
    o_ref[...] = x + _dot(merged.astype(BF16), wout_ref[...])


def _merge(x, g, wg, wkv_f, wkv_b, bonus, rgate, lnx_g, lnx_b, ones_bd, yb, yc, woa, wob, woc, wout):
    t, d = x.shape
    tm = min(TOKEN_TILE, t)
    row = lambda w_: pl.BlockSpec((tm, w_), lambda i: (i, 0))
    half = row(RWKV_DIM)
    return pl.pallas_call(
        _merge_body,
        grid=(t // tm,),
        in_specs=[row(d), _resident((1, d)), _resident(wg.shape), half, half, half, half,
                  _resident(lnx_g.shape), _resident(lnx_b.shape), _resident(ones_bd.shape), half, half,
                  _resident(woa.shape), _resident(wob.shape), _resident(woc.shape), _resident(wout.shape)],
        out_specs=row(d),
        out_shape=jax.ShapeDtypeStruct((t, d), F32),
        compiler_params=_params("parallel"),
        name="merge",
    )(x, g, wg, wkv_f, wkv_b, bonus, rgate, lnx_g, lnx_b, ones_bd, yb, yc, woa, wob, woc, wout)


def _rope_tables(seq):
    t = jnp.arange(seq)

    def angles(pos, dim):
        inv_freq = ROPE_THETA ** (-jnp.arange(0, dim, 2, dtype=F32) / dim)
        return pos.astype(F32)[:, None] * inv_freq[None, :]

    ang_row, ang_col = angles(t // GRID_W, HEAD_DIM // 2), angles(t % GRID_W, HEAD_DIM // 2)
    ang_1d = angles(t, HEAD_DIM)
    ga_cos = jnp.concatenate([jnp.cos(ang_row)] * 2 + [jnp.cos(ang_col)] * 2, axis=1)
    ga_sin = jnp.concatenate([-jnp.sin(ang_row), jnp.sin(ang_row), -jnp.sin(ang_col), jnp.sin(ang_col)], axis=1)
    wa_cos = jnp.concatenate([jnp.cos(ang_1d)] * 2, axis=1)
    wa_sin = jnp.concatenate([-jnp.sin(ang_1d), jnp.sin(ang_1d)], axis=1)
    return tuple(jnp.concatenate([a, a], axis=1) for a in (ga_cos, ga_sin, wa_cos, wa_sin))


def _two_way(w):
    z = jnp.zeros_like(w[0])
    return jnp.concatenate([jnp.concatenate([w[0], z], axis=1), jnp.concatenate([z, w[1]], axis=1)], axis=0)


def kernel(x, ffn1_norm, ffn1_w_up, ffn1_w_down, mix_norm, w_in, rwkv_mu, rwkv_w0, rwkv_w2, rwkv_a0, rwkv_a2, rwkv_g2, rwkv_k_k, rwkv_k_a, rwkv_r_k, rwkv_lnx_g, rwkv_lnx_b, ga_q_norm, ga_k_norm, wa_sink, w_o_rwkv, w_o_ga, w_o_wa, w_out, ffn2_norm, ffn2_w_up, ffn2_w_down, final_norm):
    bsz, seq, dm = x.shape
    depth = w_in.shape[0]
    tokens = bsz * seq
    tabs = _rope_tables(seq)
    idx = jnp.arange(512)
    ones_bd = (idx[:, None] // HEAD_DIM == idx[None, :] // HEAD_DIM).astype(BF16)
    row1 = lambda a: a.reshape(1, -1)
    final_g = row1(final_norm)

    xt = x.reshape(tokens, dm)
    for l in range(depth):
        xt = _ffn(xt, row1(ffn1_norm[l]), ffn1_w_up[l].astype(BF16), ffn1_w_down[l].astype(BF16), final_g, False)

        w_mix = w_in[l, :, :MIX_COLS].astype(BF16)
        w_gate = w_in[l, :, MIX_COLS:].astype(BF16)
        p_rwkv, ga_q, ga_kv, wa_q, wa_kv = _inproj(
            xt, row1(mix_norm[l]), w_mix, ones_bd, row1(jnp.tile(ga_q_norm[l], 8)), row1(jnp.tile(ga_k_norm[l], 8)),
            tabs, seq)

        outs = _rwkv_prep(
            p_rwkv.reshape(bsz, seq, RWKV_COLS), row1(rwkv_mu[l]), row1(rwkv_w0[l]), _two_way(rwkv_w2[l]).astype(BF16),
            row1(rwkv_a0[l]), _two_way(rwkv_a2[l]).astype(BF16), rwkv_g2[l].astype(BF16), row1(rwkv_k_k[l]),
            row1(rwkv_k_a[l]), row1(rwkv_r_k[l]), ones_bd)
        v16, bonus, gate = outs[16:19]
        y_f, y_b = _rwkv_scan(outs[0:8], outs[8:16], v16)
        flat = lambda a: a.reshape(tokens, RWKV_DIM)

        y_g = _global_attention(ga_q.reshape(bsz, seq, 512), ga_kv.reshape(bsz, seq, 512))
        y_w = _window_attention(wa_q.reshape(bsz, seq, 512), wa_kv.reshape(bsz, seq, 512), wa_sink[l])

        xt = _merge(xt, row1(mix_norm[l]), w_gate, flat(y_f), flat(y_b), flat(bonus), flat(gate),
                    row1(rwkv_lnx_g[l]), row1(rwkv_lnx_b[l]), ones_bd, flat(y_g), flat(y_w),
                    w_o_rwkv[l].astype(BF16), w_o_ga[l].astype(BF16), w_o_wa[l].astype(BF16), w_out[l].astype(BF16))

        xt = _ffn(xt, row1(ffn2_norm[l]), ffn2_w_up[l].astype(BF16), ffn2_w_down[l].astype(BF16), final_g,
                  l == depth - 1)
    return xt.reshape(bsz, seq, dm)
```

```python
import functools

import jax
import jax.numpy as jnp
import numpy as np
from jax import lax
from jax.experimental import pallas as pl
from jax.experimental.pallas import tpu as pltpu

F32 = jnp.float32
BF16 = jnp.bfloat16

D_MODEL = 1024
HEAD_DIM = 64
RWKV_DIM = 512
RWKV_COLS = 1920
GA_COLS = 768
WA_COLS = 768
MIX_COLS = RWKV_COLS + GA_COLS + WA_COLS
D_FF = 2816
GRID_W = 64
WINDOW = 128
ROPE_THETA = 10000.0
NORM_EPS = 1e-6
LNX_EPS = HEAD_DIM * 1e-5
NEG_INF = -1e30
ATTN_SCALE = HEAD_DIM ** -0.5

CHUNK = 64
QUAD = 4 * HEAD_DIM
LANES = 128
VMEM_LIMIT = 56 * 1024 * 1024

TOKEN_TILE = 512
FFN_TILE = 1024
ROW_PART = 256
INPROJ_PART = 128
FF_CHUNK = 2816
PREP_TILE = 512
PREP_SUB = 128
SCAN_CHUNKS = 4
SCAN_BATCH = 2
GA_Q_TILE = 256
WA_Q_TILE = 256


def _params(*sem):
    return pltpu.CompilerParams(dimension_semantics=sem, vmem_limit_bytes=VMEM_LIMIT)


def _resident(shape):
    nd = len(shape)
    return pl.BlockSpec(shape, lambda *_: (0,) * nd, pipeline_mode=pl.Buffered(1))


def _rmsnorm(x, g):
    return x * lax.rsqrt(jnp.mean(x * x, axis=-1, keepdims=True) + NORM_EPS) * g


def _row_halves(rows, part=None):
    part = min(part or ROW_PART, rows // 2) if rows % 32 == 0 else rows
    return [slice(r, r + part) for r in range(0, rows, part)]


def _dot(a, b):
    return jnp.dot(a, b, preferred_element_type=F32)


def _dot_nt(a, b):
    return lax.dot_general(a, b, (((1,), (1,)), ((), ())), preferred_element_type=F32)


def _dot_tn(a, b):
    return lax.dot_general(a, b, (((0,), (0,)), ((), ())), preferred_element_type=F32)


def _split_bf16(x, terms):
    out = []
    for _ in range(terms - 1):
        hi = x.astype(BF16)
        out.append(hi)
        x = x - hi.astype(F32)
    out.append(x.astype(BF16))
    return out


def _head_sum(x, ones_bd, terms):
    w = x.shape[-1]
    blk = min(w, QUAD)
    ones = ones_bd[:blk, :blk]
    parts = []
    for c0 in range(0, w, blk):
        acc = None
        for t in _split_bf16(x[:, c0:c0 + blk], terms):
            y = _dot(t, ones)
            acc = y if acc is None else acc + y
        parts.append(acc)
    return parts[0] if len(parts) == 1 else jnp.concatenate(parts, axis=1)


def _ffn_body(x_ref, g_ref, wup_ref, wdn_ref, gf_ref, o_ref, act_ref, *, final_norm):
    halves = _row_halves(x_ref.shape[0])
    h = [_rmsnorm(x_ref[rs, :], g_ref[...]).astype(BF16) for rs in halves]
    for c in range(D_FF // FF_CHUNK):
        lo, hi = c * FF_CHUNK, (c + 1) * FF_CHUNK
        for rs, hh in zip(halves, h):
            gate = _dot(hh, wup_ref[:, lo:hi])
            up = _dot(hh, wup_ref[:, D_FF + lo:D_FF + hi])
            act_ref[rs, lo:hi] = (gate * jax.nn.sigmoid(gate) * up).astype(BF16)
    for rs in halves:
        y = x_ref[rs, :] + 0.5 * _dot(act_ref[rs, :], wdn_ref[...])
        if final_norm:
            y = _rmsnorm(y, gf_ref[...])
        o_ref[rs, :] = y


def _ffn(x, g, w_up, w_down, g_final, final_norm):
    t, d = x.shape
    tm = min(FFN_TILE, t)
    row = pl.BlockSpec((tm, d), lambda i: (i, 0))
    return pl.pallas_call(
        functools.partial(_ffn_body, final_norm=final_norm),
        grid=(t // tm,),
        in_specs=[row, _resident((1, d)), _resident(w_up.shape), _resident(w_down.shape), _resident((1, d))],
        out_specs=row,
        out_shape=jax.ShapeDtypeStruct((t, d), F32),
        scratch_shapes=[pltpu.VMEM((tm, D_FF), BF16)],
        compiler_params=_params("parallel"),
        name="ffn_final" if final_norm else "ffn",
    )(x, g, w_up, w_down, g_final)


def _rope(t, cos, sin_signed, half):
    w = t.shape[-1]
    lane = lax.broadcasted_iota(jnp.int32, t.shape, 1)
    first = (lane % (2 * half)) < half
    partner = jnp.where(first, pltpu.roll(t, w - half, 1), pltpu.roll(t, half, 1))
    return t * cos + partner * sin_signed


def _dup_heads(t):
    lane = lax.broadcasted_iota(jnp.int32, t.shape, 1)
    swapped = pltpu.roll(t, HEAD_DIM, 1)
    low = lane < HEAD_DIM
    return jnp.where(low, t, swapped), jnp.where(low, swapped, t)


def _inproj_body(x_ref, g_ref, w_ref, ones_ref, gq_ref, gk_ref, cga_ref, sga_ref, cwa_ref, swa_ref,
                 prw_ref, gaq_ref, gakv_ref, waq_ref, wakv_ref):
    ones = ones_ref[...]

    def head_norm(t, gain):
        ms = _head_sum(t * t, ones, 1) * (1.0 / HEAD_DIM)
        return t * lax.rsqrt(ms + NORM_EPS) * gain

    tile4 = lambda tab: jnp.concatenate([tab] * 4, axis=1)

    def project(rs):
        h = _rmsnorm(x_ref[rs, :], g_ref[...]).astype(BF16)
        prw_ref[rs, :] = _dot(h, w_ref[:, :RWKV_COLS])
        return _dot(h, w_ref[:, RWKV_COLS:RWKV_COLS + GA_COLS]), _dot(h, w_ref[:, RWKV_COLS + GA_COLS:MIX_COLS])

    def finish(rs, ga, wa):
        cga, sga = cga_ref[rs, :], sga_ref[rs, :]
        q = _rope(head_norm(ga[:, :512], gq_ref[...]), tile4(cga), tile4(sga), 16)
        k = _rope(head_norm(ga[:, 512:640], gk_ref[:, :LANES]), cga, sga, 16)
        gaq_ref[rs, :] = (q * ATTN_SCALE).astype(BF16)
        k0, k1 = _dup_heads(k)
        v0, v1 = _dup_heads(ga[:, 640:768])
        gakv_ref[rs, :] = jnp.concatenate([k0, k1, v0, v1], axis=1).astype(BF16)
        cwa, swa = cwa_ref[rs, :], swa_ref[rs, :]
        q = _rope(wa[:, :512], tile4(cwa), tile4(swa), 32)
        k = _rope(wa[:, 512:640], cwa, swa, 32)
        waq_ref[rs, :] = (q * ATTN_SCALE).astype(BF16)
        k0, k1 = _dup_heads(k)
        v0, v1 = _dup_heads(wa[:, 640:768])
        wakv_ref[rs, :] = jnp.concatenate([k0, k1, v0, v1], axis=1).astype(BF16)

    parts = _row_halves(x_ref.shape[0], INPROJ_PART)
    projected = {}
    for j in range(len(parts) + 1):
        if j < len(parts):
            projected[j] = project(parts[j])
        if j >= 1:
            finish(parts[j - 1], *projected.pop(j - 1))


def _inproj(x, g, w, ones_bd, gq, gk, tabs, seq):
    t, d = x.shape
    tm = min(TOKEN_TILE, seq)
    per_seq = seq // tm
    row = lambda w_: pl.BlockSpec((tm, w_), lambda i: (i, 0))
    tab = pl.BlockSpec((tm, LANES), lambda i: (i % per_seq, 0))
    return pl.pallas_call(
        _inproj_body,
        grid=(t // tm,),
        in_specs=[row(d), _resident((1, d)), _resident(w.shape), _resident(ones_bd.shape),
                  _resident(gq.shape), _resident(gk.shape), tab, tab, tab, tab],
        out_specs=[row(RWKV_COLS), row(512), row(512), row(512), row(512)],
        out_shape=[jax.ShapeDtypeStruct((t, RWKV_COLS), F32)] + [jax.ShapeDtypeStruct((t, 512), BF16)] * 4,
        compiler_params=_params("parallel"),
        name="inproj",
    )(x, g, w, ones_bd, gq, gk, *tabs)


def _block_diag4(x):
    head = lax.broadcasted_iota(jnp.int32, x.shape, 1) // HEAD_DIM
    return jnp.concatenate([jnp.where(head == h, x, jnp.zeros_like(x)) for h in range(4)], axis=0)


def _bd16(x):
    return _block_diag4(x).astype(BF16)


def _rwkv_prep_body(p_ref, hp_ref, hn_ref, mu_ref, w0_ref, w2_ref, a0_ref, a2_ref, g2_ref, kk_ref, ka_ref, rk_ref,
                    ones_ref, *out_refs, ts):
    i = pl.program_id(1)
    n = pl.num_programs(1)
    edge_prev = jnp.where(i > 0, hp_ref[0, 7:8, :], 0.0)
    edge_next = jnp.where(i < n - 1, hn_ref[0, 0:1, :], 0.0)
    consts = (mu_ref, w0_ref, w2_ref, a0_ref, a2_ref, g2_ref, kk_ref, ka_ref, rk_ref, ones_ref)
    chain_stage = None
    for r0 in range(0, ts, PREP_SUB):
        prev_row = edge_prev if r0 == 0 else p_ref[0, r0 - 1:r0, :]
        next_row = edge_next if r0 + PREP_SUB == ts else p_ref[0, r0 + PREP_SUB:r0 + PREP_SUB + 1, :]
        state = {}
        vector_stage = _prep_vector_stage(p_ref, prev_row, next_row, consts, out_refs, r0, PREP_SUB, state)
        _alternate(chain_stage, vector_stage)
        chain_stage = _prep_chain_stage(state, out_refs, r0, PREP_SUB)
    _alternate(chain_stage, None)


def _alternate(first, second):
    live = [g for g in (first, second) if g is not None]
    while live:
        for g in list(live):
            try:
                next(g)
            except StopIteration:
                live.remove(g)


def _prep_vector_stage(p_ref, prev_row, next_row, consts, out_refs, r0, ts, state):
    mu_ref, w0_ref, w2_ref, a0_ref, a2_ref, g2_ref, kk_ref, ka_ref, rk_ref, ones_ref = consts
    dir_refs = (out_refs[0:8], out_refs[8:16])
    v_ref, bonus_ref, gate_ref = out_refs[16:19]
    rows = slice(r0, r0 + ts)

    def shifted_mix(lo, hi):
        p = p_ref[0, rows, lo:hi]
        row = lax.broadcasted_iota(jnp.int32, p.shape, 0)
        prev = jnp.where(row == 0, prev_row[:, lo:hi], pltpu.roll(p, 1, 0))
        nxt = jnp.where(row == ts - 1, next_row[:, lo:hi], pltpu.roll(p, ts - 1, 0))
        return p + mu_ref[:, lo:hi] * (0.5 * (prev + nxt) - p)

    lora = shifted_mix(1536, 1920)
    wl, al, gl = lora[:, 0:128], lora[:, 128:256], lora[:, 256:384]
    yield
    r = shifted_mix(0, 512)
    yield
    k = shifted_mix(512, 1024)
    yield
    v = shifted_mix(1024, 1536)
    yield
    ones = ones_ref[...]

    u = w0_ref[...] + _dot(jnp.tanh(wl).astype(BF16), w2_ref[...])
    logw = (-float(np.exp(-0.5))) * jax.nn.sigmoid(u)
    asig = jax.nn.sigmoid(a0_ref[...] + _dot(al.astype(BF16), a2_ref[...]))
    yield
    kkr = k * kk_ref[...]
    kk = kkr * lax.rsqrt(jnp.maximum(_head_sum(kkr * kkr, ones, 1), 1e-24))
    k_a = ka_ref[...]
    kd = [k * (1.0 + (asig[:, d * 512:(d + 1) * 512] - 1.0) * k_a) for d in range(2)]
    yield
    bonus_ref[0, rows] = (_head_sum(r * rk_ref[...] * (kd[0] + kd[1]), ones, 1) * v).astype(BF16)
    gate_ref[0, rows] = _dot(jax.nn.sigmoid(gl).astype(BF16), g2_ref[...]).astype(BF16)
    v_ref[0, rows] = v.astype(BF16)
    yield

    rr = lax.broadcasted_iota(jnp.int32, (ts, ts), 0)
    cc = lax.broadcasted_iota(jnp.int32, (ts, ts), 1)
    same = (rr // CHUNK) == (cc // CHUNK)

    scaled = []
    for d in range(2):
        bp_ref, kp_ref, pt_ref = dir_refs[d][5:8]
        rt_ref = dir_refs[d][1]
        before = (cc <= rr) if d == 0 else (cc >= rr)
        lhs = jnp.where(same & before, 1.0, 0.0).astype(BF16)
        lw = logw[:, d * 512:(d + 1) * 512]
        cl = None
        for term in _split_bf16(lw, 2):
            y = _dot(lhs, term)
            cl = y if cl is None else cl + y
        yield
        ends = [c * CHUNK + (CHUNK - 1 if d == 0 else 0) for c in range(ts // CHUNK)]
        p_tot = [jnp.exp(cl[e:e + 1, :]) for e in ends]
        for c in range(ts // CHUNK):
            pt_ref[0, r0 // CHUNK + c] = p_tot[c]
        e_neg = jnp.exp(-cl)
        b_raw = kk * asig[:, d * 512:(d + 1) * 512]
        r_t = r * jnp.exp(cl)
        rt_ref[0, rows] = r_t.astype(BF16)
        yield
        e_out = e_neg * jnp.concatenate([jnp.broadcast_to(pc, (CHUNK, RWKV_DIM)) for pc in p_tot], axis=0)
        bp_ref[0, rows] = (b_raw * e_out).astype(BF16)
        kp_ref[0, rows] = (kd[d] * e_out).astype(BF16)
        yield
        scaled.append((-kk * jnp.exp(cl - lw), b_raw * e_neg, kd[d] * e_neg, r_t))
        yield
    state.update(scaled=scaled, v=v)


def _prep_chain_stage(state, out_refs, r0, ts):
    scaled, v = state["scaled"], state["v"]
    dir_refs = (out_refs[0:8], out_refs[8:16])
    tt = lax.broadcasted_iota(jnp.int32, (CHUNK, QUAD), 0)
    ss = lax.broadcasted_iota(jnp.int32, (CHUNK, QUAD), 1) % CHUNK
    eye = jnp.where(tt == ss, 1.0, 0.0)
    chains = [(d, c, qd) for d in range(2) for c in range(ts // CHUNK) for qd in range(RWKV_DIM // QUAD)]
    cut = lambda arr, c, qd: arr[c * CHUNK:(c + 1) * CHUNK, qd * QUAD:(qd + 1) * QUAD]
    strict = [ss < tt, ss > tt]
    incl = [ss <= tt, ss >= tt]
    at, x2 = {}, {}
    for ch in chains:
        d, c, qd = ch
        at[ch] = cut(scaled[d][0], c, qd)
        x2[ch] = jnp.concatenate([at[ch], cut(scaled[d][3], c, qd)], axis=0).astype(BF16)
    w_b = {ch: _dot_nt(x2[ch], _bd16(cut(scaled[ch[0]][1], ch[1], ch[2]))) for ch in chains}
    yield
    w_k = {ch: _dot_nt(x2[ch], _bd16(cut(scaled[ch[0]][2], ch[1], ch[2]))) for ch in chains}
    yield
    a_ab, gy = {}, {}
    for ch in chains:
        d, c, qd = ch
        a_ab[ch] = jnp.where(strict[d], w_b[ch][:CHUNK], 0.0)
        dir_refs[d][2][0, r0 + c * CHUNK:r0 + (c + 1) * CHUNK, qd * QUAD:(qd + 1) * QUAD] = (
            jnp.where(incl[d], w_b[ch][CHUNK:], 0.0).astype(BF16))
        lk = jnp.concatenate([jnp.where(strict[d], w_k[ch][:CHUNK], 0.0),
                              jnp.where(incl[d], w_k[ch][CHUNK:], 0.0)], axis=0)
        gy[ch] = _dot(lk.astype(BF16), _bd16(cut(v, c, qd)))
    yield
    tm = {ch: eye + a_ab[ch] for ch in chains}
    pw = {ch: _dot(a_ab[ch].astype(BF16), _bd16(a_ab[ch])) for ch in chains}
    yield
    for _ in range(4):
        both = {ch: _dot(jnp.concatenate([pw[ch], tm[ch]], axis=0).astype(BF16), _bd16(pw[ch])) for ch in chains}
        tm = {ch: tm[ch] + both[ch][CHUNK:] for ch in chains}
        pw = {ch: both[ch][:CHUNK] for ch in chains}
        yield
    tm = {ch: (tm[ch] + _dot(tm[ch].astype(BF16), _bd16(pw[ch]))).astype(BF16) for ch in chains}
    yield
    for ch in chains:
        d, c, qd = ch
        rs, ls = slice(r0 + c * CHUNK, r0 + (c + 1) * CHUNK), slice(qd * QUAD, (qd + 1) * QUAD)
        at_ref, _, _, uv_ref, yv_ref = dir_refs[d][0:5]
        uv_ref[0, rs, ls] = _dot(tm[ch], _bd16(gy[ch][:CHUNK])).astype(BF16)
        at_ref[0, rs, ls] = _dot(tm[ch], _bd16(at[ch])).astype(BF16)
        yv_ref[0, rs, ls] = gy[ch][CHUNK:].astype(BF16)


def _rwkv_prep(p, mu, w0, w2bd, a0, a2bd, g2, k_k, k_a, r_k, ones_bd):
    b, s, _ = p.shape
    ts = min(PREP_TILE, s)
    nt = s // ts
    h8 = ts // 8
    main = pl.BlockSpec((1, ts, RWKV_COLS), lambda bi, i: (bi, i, 0))
    halo_prev = pl.BlockSpec((1, 8, RWKV_COLS), lambda bi, i: (bi, jnp.maximum(i * h8 - 1, 0), 0))
    halo_next = pl.BlockSpec((1, 8, RWKV_COLS), lambda bi, i: (bi, jnp.minimum((i + 1) * h8, s // 8 - 1), 0))
    tok = pl.BlockSpec((1, ts, RWKV_DIM), lambda bi, i: (bi, i, 0))
    ptot = pl.BlockSpec((1, ts // CHUNK, 1, RWKV_DIM), lambda bi, i: (bi, i, 0, 0))
    tok16 = jax.ShapeDtypeStruct((b, s, RWKV_DIM), BF16)
    tok32 = jax.ShapeDtypeStruct((b, s, RWKV_DIM), F32)
    pt32 = jax.ShapeDtypeStruct((b, s // CHUNK, 1, RWKV_DIM), F32)
    per_dir_specs = [tok, tok, tok, tok, tok, tok, tok, ptot]
    per_dir_shapes = [tok16] * 7 + [pt32]
    consts = [mu, w0, w2bd, a0, a2bd, g2, k_k, k_a, r_k, ones_bd]
    return pl.pallas_call(
        functools.partial(_rwkv_prep_body, ts=ts),
        grid=(b, nt),
        in_specs=[main, halo_prev, halo_next] + [_resident(c.shape) for c in consts],
        out_specs=per_dir_specs * 2 + [tok, tok, tok],
        out_shape=per_dir_shapes * 2 + [tok16, tok16, tok16],
        compiler_params=_params("parallel", "parallel"),
        name="rwkv_prep",
    )(p, p, p, *consts)


def _rwkv_scan_body(*refs, nc, nb):
    fwd, bwd = refs[0:8], refs[8:16]
    vf_ref, vb_ref = refs[16:18]
    yf_ref, yb_ref = refs[18:20]
    st_ref = refs[20]
    step = pl.program_id(1)

    @pl.when(step == 0)
    def _():
        st_ref[...] = jnp.zeros_like(st_ref)

    rr = lax.broadcasted_iota(jnp.int32, (QUAD, QUAD), 0) // HEAD_DIM
    cc = lax.broadcasted_iota(jnp.int32, (QUAD, QUAD), 1) // HEAD_DIM
    diag = rr == cc
    dirs = ((fwd, vf_ref, yf_ref), (bwd, vb_ref, yb_ref))
    chains = [(bi, d, qd) for bi in range(nb) for d in range(2) for qd in range(RWKV_DIM // QUAD)]
    st = {ch: st_ref[ch] for ch in chains}
    for j in range(nc):
        z = {}
        for ch in chains:
            bi, d, qd = ch
            c = j if d == 0 else nc - 1 - j
            rs, ls = slice(c * CHUNK, (c + 1) * CHUNK), slice(qd * QUAD, (qd + 1) * QUAD)
            at_ref, rt_ref = dirs[d][0][0:2]
            x2 = jnp.concatenate([at_ref[bi, rs, ls], rt_ref[bi, rs, ls]], axis=0)
            z[ch] = _dot_nt(x2, st[ch].astype(BF16))
        for ch in chains:
            bi, d, qd = ch
            c = j if d == 0 else nc - 1 - j
            rs, ls = slice(c * CHUNK, (c + 1) * CHUNK), slice(qd * QUAD, (qd + 1) * QUAD)
            (_, _, mrb_ref, uv_ref, yv_ref, bp_ref, kp_ref, pt_ref), v_ref, y_ref = dirs[d]
            u = z[ch][:CHUNK] + uv_ref[bi, rs, ls]
            y = z[ch][CHUNK:] + _dot(mrb_ref[bi, rs, ls], _bd16(u)) + yv_ref[bi, rs, ls]
            y_ref[bi, rs, ls] = y.astype(BF16)
            uv2 = jnp.concatenate([u.astype(BF16), v_ref[bi, rs, ls]], axis=0)
            bk2 = jnp.concatenate([bp_ref[bi, rs, ls], kp_ref[bi, rs, ls]], axis=0)
            st[ch] = st[ch] * pt_ref[bi, c][:, ls] + jnp.where(diag, _dot_tn(uv2, bk2), 0.0)
    for ch in chains:
        st_ref[ch] = st[ch]


def _rwkv_scan(fwd_arrs, bwd_arrs, v16):
    b, s, _ = v16.shape
    nc = min(SCAN_CHUNKS, s // CHUNK)
    nb = SCAN_BATCH if b % SCAN_BATCH == 0 else 1
    rows = nc * CHUNK
    steps = s // rows
    f_tok = pl.BlockSpec((nb, rows, RWKV_DIM), lambda bi, i: (bi, i, 0))
    b_tok = pl.BlockSpec((nb, rows, RWKV_DIM), lambda bi, i: (bi, steps - 1 - i, 0))
    f_pt = pl.BlockSpec((nb, nc, 1, RWKV_DIM), lambda bi, i: (bi, i, 0, 0))
    b_pt = pl.BlockSpec((nb, nc, 1, RWKV_DIM), lambda bi, i: (bi, steps - 1 - i, 0, 0))
    y_shape = jax.ShapeDtypeStruct((b, s, RWKV_DIM), BF16)
    return pl.pallas_call(
        functools.partial(_rwkv_scan_body, nc=nc, nb=nb),
        grid=(b // nb, steps),
        in_specs=[f_tok] * 7 + [f_pt] + [b_tok] * 7 + [b_pt] + [f_tok, b_tok],
        out_specs=[f_tok, b_tok],
        out_shape=[y_shape, y_shape],
        scratch_shapes=[pltpu.VMEM((nb, 2, RWKV_DIM // QUAD, QUAD, QUAD), F32)],
        compiler_params=_params("parallel", "arbitrary"),
        name="rwkv_scan",
    )(*fwd_arrs, *bwd_arrs, v16, v16)


N_PAIRS = 4


def _pair_queries(q_ref, pair):
    qp = q_ref[0, :, pair * LANES:(pair + 1) * LANES]
    lane = lax.broadcasted_iota(jnp.int32, qp.shape, 1)
    zero = jnp.zeros_like(qp)
    return jnp.concatenate([jnp.where(lane < HEAD_DIM, qp, zero), jnp.where(lane >= HEAD_DIM, qp, zero)], axis=0)


def _store_pair(o_ref, pair, o, tq):
    lane = lax.broadcasted_iota(jnp.int32, (tq, LANES), 1)
    o_ref[0, :, pair * LANES:(pair + 1) * LANES] = jnp.where(lane < HEAD_DIM, o[:tq], o[tq:]).astype(BF16)


def _values_and_ones(vd):
    return jnp.concatenate([vd, jnp.ones_like(vd)], axis=1)


def _skewed_pairs(scores, softmax, finish):
    sc, pr = {}, {}
    for step in range(N_PAIRS + 2):
        if step < N_PAIRS:
            sc[step] = scores(step)
        if 1 <= step <= N_PAIRS:
            pr[step - 1] = softmax(step - 1, sc.pop(step - 1))
        if step >= 2:
            finish(step - 2, pr.pop(step - 2))


def _ga_body(q_ref, kv_ref, o_ref, *, tq):
    def scores(p):
        return _dot_nt(_pair_queries(q_ref, p), kv_ref[0, :, (p // 2) * LANES:(p // 2 + 1) * LANES])

    def softmax(p, sc):
        return jnp.exp(sc - jnp.max(sc, axis=-1, keepdims=True)).astype(BF16)

    def finish(p, e):
        ov = _dot(e, _values_and_ones(kv_ref[0, :, (2 + p // 2) * LANES:(3 + p // 2) * LANES]))
        _store_pair(o_ref, p, ov[:, :LANES] / ov[:, LANES:], tq)

    _skewed_pairs(scores, softmax, finish)


def _global_attention(q, kv):
    b, s, w = q.shape
    tq = min(GA_Q_TILE, s)
    qs = pl.BlockSpec((1, tq, w), lambda bi, i: (bi, i, 0))
    kvs = pl.BlockSpec((1, s, w), lambda bi, i: (bi, 0, 0))
    return pl.pallas_call(
        functools.partial(_ga_body, tq=tq),
        grid=(b, s // tq),
        in_specs=[qs, kvs],
        out_specs=qs,
        out_shape=jax.ShapeDtypeStruct((b, s, w), BF16),
        compiler_params=_params("parallel", "arbitrary"),
        name="global_attn",
    )(q, kv)


def _wa_window_start(i, tq, span, seq):
    return jnp.clip(i * tq - WINDOW, 0, seq - span)


def _wa_body(sink_ref, q_ref, kv_ref, band_ref, o_ref, *, tq, span, seq):
    start = pl.multiple_of(_wa_window_start(pl.program_id(1), tq, span, seq), WINDOW)
    valid = band_ref[0] > 0.5
    valid2 = jnp.concatenate([valid, valid], axis=0)

    def scores(p):
        kd = kv_ref[0, pl.ds(start, span), (p // 2) * LANES:(p // 2 + 1) * LANES]
        return jnp.where(valid2, _dot_nt(_pair_queries(q_ref, p), kd), NEG_INF)

    def softmax(p, sc):
        e, sink_e = [], []
        for h in range(2):
            sc_h = sc[h * tq:(h + 1) * tq]
            sink = sink_ref[2 * p + h]
            m = jnp.maximum(jnp.max(sc_h, axis=-1, keepdims=True), sink)
            e.append(jnp.exp(sc_h - m).astype(BF16))
            sink_e.append(jnp.exp(sink - m))
        return jnp.concatenate(e, axis=0), jnp.concatenate(sink_e, axis=0)

    def finish(p, es):
        e, sink_e = es
        ov = _dot(e, _values_and_ones(kv_ref[0, pl.ds(start, span), (2 + p // 2) * LANES:(3 + p // 2) * LANES]))
        _store_pair(o_ref, p, ov[:, :LANES] / (ov[:, LANES:] + sink_e), tq)

    _skewed_pairs(scores, softmax, finish)


def _window_attention(q, kv, sink):
    b, s, w = q.shape
    tq = min(WA_Q_TILE, s)
    span = min(tq + 2 * WINDOW, s)
    nblk = s // tq
    offsets = sorted({int(np.clip(i * tq - WINDOW, 0, s - span)) - i * tq for i in range(nblk)}, reverse=True)
    rel = jnp.arange(span)[None, :] - jnp.arange(tq)[:, None]
    band = jnp.stack([(jnp.abs(rel + off) <= WINDOW).astype(F32) for off in offsets])
    which = lambda i: jnp.where(i == 0, 0, jnp.where(i == nblk - 1, len(offsets) - 1, min(1, len(offsets) - 1)))
    qs = pl.BlockSpec((1, tq, w), lambda bi, i, *_: (bi, i, 0))
    kvs = pl.BlockSpec((1, s, w), lambda bi, i, *_: (bi, 0, 0))
    bands = pl.BlockSpec((1, tq, span), lambda bi, i, *_: (which(i), 0, 0))
    return pl.pallas_call(
        functools.partial(_wa_body, tq=tq, span=span, seq=s),
        grid_spec=pltpu.PrefetchScalarGridSpec(
            num_scalar_prefetch=1, grid=(b, nblk), in_specs=[qs, kvs, bands], out_specs=qs),
        out_shape=jax.ShapeDtypeStruct((b, s, w), BF16),
        compiler_params=_params("parallel", "arbitrary"),
        name="window_attn",
    )(sink, q, kv, band)


def _merge_body(x_ref, g_ref, wg_ref, wf_ref, wb_ref, bonus_ref, rgate_ref, lg_ref, lb_ref, ones_ref,
                yb_ref, yc_ref, woa_ref, wob_ref, woc_ref, wout_ref, o_ref):
    ones = ones_ref[...]

    def prepare(rs):
        h = _rmsnorm(x_ref[rs, :], g_ref[...]).astype(BF16)
        wkv = wf_ref[rs, :].astype(F32) + wb_ref[rs, :].astype(F32)
        dev = wkv - _head_sum(wkv, ones, 2) * (1.0 / HEAD_DIM)
        var = _head_sum(dev * dev, ones, 1) * (1.0 / HEAD_DIM)
        y_a = dev * lax.rsqrt(var + LNX_EPS) * lg_ref[...] + lb_ref[...] + bonus_ref[rs, :].astype(F32)
        return h, (y_a * rgate_ref[rs, :].astype(F32)).astype(BF16)

    def gated_sum(rs, h, y_a):
        gates = jax.nn.sigmoid(_dot(h, wg_ref[...]))
        merged = None
        for j, (y, wo_ref) in enumerate(((y_a, woa_ref), (yb_ref[rs, :], wob_ref), (yc_ref[rs, :], woc_ref))):
            term = gates[:, j * D_MODEL:(j + 1) * D_MODEL] * _dot(y, wo_ref[...])
            merged = term if merged is None else merged + term
        return merged.astype(BF16)

    parts = _row_halves(x_ref.shape[0])
    prepared = [prepare(rs) for rs in parts]
    merged = [gated_sum(rs, *pr) for rs, pr in zip(parts, prepared)]
    for rs, m in zip(parts, merged):
        o_ref[rs, :] = x_ref[rs, :] + _dot(m, wout_ref[...])


def _merge(x, g, wg, wkv_f, wkv_b, bonus, rgate, lnx_g, lnx_b, ones_bd, yb, yc, woa, wob, woc, wout):
    t, d = x.shape
    tm = min(TOKEN_TILE, t)
    row = lambda w_: pl.BlockSpec((tm, w_), lambda i: (i, 0))
    half = row(RWKV_DIM)
    return pl.pallas_call(
        _merge_body,
        grid=(t // tm,),
        in_specs=[row(d), _resident((1, d)), _resident(wg.shape), half, half, half, half,
                  _resident(lnx_g.shape), _resident(lnx_b.shape), _resident(ones_bd.shape), half, half,
                  _resident(woa.shape), _resident(wob.shape), _resident(woc.shape), _resident(wout.shape)],
        out_specs=row(d),
        out_shape=jax.ShapeDtypeStruct((t, d), F32),
        compiler_params=_params("parallel"),
        name="merge",
    )(x, g, wg, wkv_f, wkv_b, bonus, rgate, lnx_g, lnx_b, ones_bd, yb, yc, woa, wob, woc, wout)


def _rope_tables(seq):
    t = jnp.arange(seq)

    def angles(pos, dim):
        inv_freq = ROPE_THETA ** (-jnp.arange(0, dim, 2, dtype=F32) / dim)
        return pos.astype(F32)[:, None] * inv_freq[None, :]

    ang_row, ang_col = angles(t // GRID_W, HEAD_DIM // 2), angles(t % GRID_W, HEAD_DIM // 2)
    ang_1d = angles(t, HEAD_DIM)
    ga_cos = jnp.concatenate([jnp.cos(ang_row)] * 2 + [jnp.cos(ang_col)] * 2, axis=1)
    ga_sin = jnp.concatenate([-jnp.sin(ang_row), jnp.sin(ang_row), -jnp.sin(ang_col), jnp.sin(ang_col)], axis=1)
    wa_cos = jnp.concatenate([jnp.cos(ang_1d)] * 2, axis=1)
    wa_sin = jnp.concatenate([-jnp.sin(ang_1d), jnp.sin(ang_1d)], axis=1)
    return tuple(jnp.concatenate([a, a], axis=1) for a in (ga_cos, ga_sin, wa_cos, wa_sin))


def _two_way(w):
    z = jnp.zeros_like(w[0])
    return jnp.concatenate([jnp.concatenate([w[0], z], axis=1), jnp.concatenate([z, w[1]], axis=1)], axis=0)


def kernel(x, ffn1_norm, ffn1_w_up, ffn1_w_down, mix_norm, w_in, rwkv_mu, rwkv_w0, rwkv_w2, rwkv_a0, rwkv_a2, rwkv_g2, rwkv_k_k, rwkv_k_a, rwkv_r_k, rwkv_lnx_g, rwkv_lnx_b, ga_q_norm, ga_k_norm, wa_sink, w_o_rwkv, w_o_ga, w_o_wa, w_out, ffn2_norm, ffn2_w_up, ffn2_w_down, final_norm):
    bsz, seq, dm = x.shape
    depth = w_in.shape[0]
    tokens = bsz * seq
    tabs = _rope_tables(seq)
    idx = jnp.arange(512)
    ones_bd = (idx[:, None] // HEAD_DIM == idx[None, :] // HEAD_DIM).astype(BF16)
    row1 = lambda a: a.reshape(1, -1)
    final_g = row1(final_norm)

    xt = x.reshape(tokens, dm)
    for l in range(depth):
        xt = _ffn(xt, row1(ffn1_norm[l]), ffn1_w_up[l].astype(BF16), ffn1_w_down[l].astype(BF16), final_g, False)

        w_mix = w_in[l, :, :MIX_COLS].astype(BF16)
        w_gate = w_in[l, :, MIX_COLS:].astype(BF16)
        p_rwkv, ga_q, ga_kv, wa_q, wa_kv = _inproj(
            xt, row1(mix_norm[l]), w_mix, ones_bd, row1(jnp.tile(ga_q_norm[l], 8)), row1(jnp.tile(ga_k_norm[l], 8)),
            tabs, seq)

        outs = _rwkv_prep(
            p_rwkv.reshape(bsz, seq, RWKV_COLS), row1(rwkv_mu[l]), row1(rwkv_w0[l]), _two_way(rwkv_w2[l]).astype(BF16),
            row1(rwkv_a0[l]), _two_way(rwkv_a2[l]).astype(BF16), rwkv_g2[l].astype(BF16), row1(rwkv_k_k[l]),
            row1(rwkv_k_a[l]), row1(rwkv_r_k[l]), ones_bd)
        v16, bonus, gate = outs[16:19]
        y_f, y_b = _rwkv_scan(outs[0:8], outs[8:16], v16)
        flat = lambda a: a.reshape(tokens, RWKV_DIM)

        y_g = _global_attention(ga_q.reshape(bsz, seq, 512), ga_kv.reshape(bsz, seq, 512))
        y_w = _window_attention(wa_q.reshape(bsz, seq, 512), wa_kv.reshape(bsz, seq, 512), wa_sink[l])

        xt = _merge(xt, row1(mix_norm[l]), w_gate, flat(y_f), flat(y_b), flat(bonus), flat(gate),
                    row1(rwkv_lnx_g[l]), row1(rwkv_lnx_b[l]), ones_bd, flat(y_g), flat(y_w),
                    w_o_rwkv[l].astype(BF16), w_o_ga[l].astype(BF16), w_o_wa[l].astype(BF16), w_out[l].astype(BF16))

        xt = _ffn(xt, row1(ffn2_norm[l]), ffn2_w_up[l].astype(BF16), ffn2_w_down[l].astype(BF16), final_g,
                  l == depth - 1)
    return xt.reshape(bsz, seq, dm)
```

```python
import functools

import jax
import jax.numpy as jnp
import numpy as np
from jax import lax
from jax.experimental import pallas as pl
from jax.experimental.pallas import tpu as pltpu

F32 = jnp.float32
BF16 = jnp.bfloat16

D_MODEL = 1024
HEAD_DIM = 64
RWKV_DIM = 512
RWKV_COLS = 1920
GA_COLS = 768
WA_COLS = 768
MIX_COLS = RWKV_COLS + GA_COLS + WA_COLS
D_FF = 2816
GRID_W = 64
WINDOW = 128
ROPE_THETA = 10000.0
NORM_EPS = 1e-6
LNX_EPS = HEAD_DIM * 1e-5
NEG_INF = -1e30
LOG2E = float(np.log2(np.e))
ATTN_SCALE_LOG2 = HEAD_DIM ** -0.5 * LOG2E

CHUNK = 64
QUAD = 4 * HEAD_DIM
LANES = 128
VMEM_LIMIT = 56 * 1024 * 1024

TOKEN_TILE = 512
FFN_TILE = 1024
ROW_PART = 256
INPROJ_PART = 128
FF_CHUNK = 2816
PREP_TILE = 512
PREP_SUB = 128
SCAN_CHUNKS = 4
SCAN_BATCH = 2
GA_Q_TILE = 256
WA_Q_BLOCK = 128


def _params(*sem):
    return pltpu.CompilerParams(dimension_semantics=sem, vmem_limit_bytes=VMEM_LIMIT)


def _resident(shape):
    nd = len(shape)
    return pl.BlockSpec(shape, lambda *_: (0,) * nd, pipeline_mode=pl.Buffered(1))


def _rmsnorm(x, g):
    return x * lax.rsqrt(jnp.mean(x * x, axis=-1, keepdims=True) + NORM_EPS) * g


def _row_halves(rows, part=None):
    part = min(part or ROW_PART, rows // 2) if rows % 32 == 0 else rows
    return [slice(r, r + part) for r in range(0, rows, part)]


def _dot(a, b):
    return jnp.dot(a, b, preferred_element_type=F32)


def _dot_nt(a, b):
    return lax.dot_general(a, b, (((1,), (1,)), ((), ())), preferred_element_type=F32)


def _dot_tn(a, b):
    return lax.dot_general(a, b, (((0,), (0,)), ((), ())), preferred_element_type=F32)


def _split_bf16(x, terms):
    out = []
    for _ in range(terms - 1):
        hi = x.astype(BF16)
        out.append(hi)
        x = x - hi.astype(F32)
    out.append(x.astype(BF16))
    return out


def _head_sum(x, ones_bd, terms):
    w = x.shape[-1]
    blk = min(w, QUAD)
    ones = ones_bd[:blk, :blk]
    parts = []
    for c0 in range(0, w, blk):
        acc = None
        for t in _split_bf16(x[:, c0:c0 + blk], terms):
            y = _dot(t, ones)
            acc = y if acc is None else acc + y
        parts.append(acc)
    return parts[0] if len(parts) == 1 else jnp.concatenate(parts, axis=1)


def _ffn_body(x_ref, g_ref, wup_ref, wdn_ref, gf_ref, o_ref, act_ref, *, final_norm):
    halves = _row_halves(x_ref.shape[0])
    h = [_rmsnorm(x_ref[rs, :], g_ref[...]).astype(BF16) for rs in halves]
    for c in range(D_FF // FF_CHUNK):
        lo, hi = c * FF_CHUNK, (c + 1) * FF_CHUNK
        for rs, hh in zip(halves, h):
            gate = _dot(hh, wup_ref[:, lo:hi])
            up = _dot(hh, wup_ref[:, D_FF + lo:D_FF + hi])
            act_ref[rs, lo:hi] = (gate * jax.nn.sigmoid(gate) * up).astype(BF16)
    for rs in halves:
        y = x_ref[rs, :] + 0.5 * _dot(act_ref[rs, :], wdn_ref[...])
        if final_norm:
            y = _rmsnorm(y, gf_ref[...])
        o_ref[rs, :] = y


def _ffn(x, g, w_up, w_down, g_final, final_norm):
    t, d = x.shape
    tm = min(FFN_TILE, t)
    row = pl.BlockSpec((tm, d), lambda i: (i, 0))
    return pl.pallas_call(
        functools.partial(_ffn_body, final_norm=final_norm),
        grid=(t // tm,),
        in_specs=[row, _resident((1, d)), _resident(w_up.shape), _resident(w_down.shape), _resident((1, d))],
        out_specs=row,
        out_shape=jax.ShapeDtypeStruct((t, d), F32),
        scratch_shapes=[pltpu.VMEM((tm, D_FF), BF16)],
        compiler_params=_params("parallel"),
        name="ffn_final" if final_norm else "ffn",
    )(x, g, w_up, w_down, g_final)


def _rope(t, cos, sin_signed, half):
    w = t.shape[-1]
    lane = lax.broadcasted_iota(jnp.int32, t.shape, 1)
    first = (lane % (2 * half)) < half
    partner = jnp.where(first, pltpu.roll(t, w - half, 1), pltpu.roll(t, half, 1))
    return t * cos + partner * sin_signed


def _dup_heads(t):
    lane = lax.broadcasted_iota(jnp.int32, t.shape, 1)
    swapped = pltpu.roll(t, HEAD_DIM, 1)
    low = lane < HEAD_DIM
    return jnp.where(low, t, swapped), jnp.where(low, swapped, t)


def _inproj_body(x_ref, g_ref, w_ref, ones_ref, gq_ref, gk_ref, cga_ref, sga_ref, cwa_ref, swa_ref,
                 prw_ref, gaq_ref, gakv_ref, waq_ref, wakv_ref):
    ones = ones_ref[...]

    def head_norm(t, gain):
        ms = _head_sum(t * t, ones, 1) * (1.0 / HEAD_DIM)
        return t * lax.rsqrt(ms + NORM_EPS) * gain

    tile4 = lambda tab: jnp.concatenate([tab] * 4, axis=1)

    def project(rs):
        h = _rmsnorm(x_ref[rs, :], g_ref[...]).astype(BF16)
        prw_ref[rs, :] = _dot(h, w_ref[:, :RWKV_COLS])
        return _dot(h, w_ref[:, RWKV_COLS:RWKV_COLS + GA_COLS]), _dot(h, w_ref[:, RWKV_COLS + GA_COLS:MIX_COLS])

    def finish(rs, ga, wa):
        cga, sga = cga_ref[rs, :], sga_ref[rs, :]
        q = _rope(head_norm(ga[:, :512], gq_ref[...]), tile4(cga), tile4(sga), 16)
        k = _rope(head_norm(ga[:, 512:640], gk_ref[:, :LANES]), cga, sga, 16)
        gaq_ref[rs, :] = (q * ATTN_SCALE_LOG2).astype(BF16)
        k0, k1 = _dup_heads(k)
        v0, v1 = _dup_heads(ga[:, 640:768])
        gakv_ref[rs, :] = jnp.concatenate([k0, k1, v0, v1], axis=1).astype(BF16)
        cwa, swa = cwa_ref[rs, :], swa_ref[rs, :]
        q = _rope(wa[:, :512], tile4(cwa), tile4(swa), 32)
        k = _rope(wa[:, 512:640], cwa, swa, 32)
        waq_ref[rs, :] = (q * ATTN_SCALE_LOG2).astype(BF16)
        k0, k1 = _dup_heads(k)
        v0, v1 = _dup_heads(wa[:, 640:768])
        wakv_ref[rs, :] = jnp.concatenate([k0, k1, v0, v1], axis=1).astype(BF16)

    parts = _row_halves(x_ref.shape[0], INPROJ_PART)
    projected = {}
    for j in range(len(parts) + 1):
        if j < len(parts):
            projected[j] = project(parts[j])
        if j >= 1:
            finish(parts[j - 1], *projected.pop(j - 1))


def _inproj(x, g, w, ones_bd, gq, gk, tabs, seq):
    t, d = x.shape
    tm = min(TOKEN_TILE, seq)
    per_seq = seq // tm
    row = lambda w_: pl.BlockSpec((tm, w_), lambda i: (i, 0))
    tab = pl.BlockSpec((tm, LANES), lambda i: (i % per_seq, 0))
    return pl.pallas_call(
        _inproj_body,
        grid=(t // tm,),
        in_specs=[row(d), _resident((1, d)), _resident(w.shape), _resident(ones_bd.shape),
                  _resident(gq.shape), _resident(gk.shape), tab, tab, tab, tab],
        out_specs=[row(RWKV_COLS), row(512), row(512), row(512), row(512)],
        out_shape=[jax.ShapeDtypeStruct((t, RWKV_COLS), F32)] + [jax.ShapeDtypeStruct((t, 512), BF16)] * 4,
        compiler_params=_params("parallel"),
        name="inproj",
    )(x, g, w, ones_bd, gq, gk, *tabs)


def _block_diag4(x):
    head = lax.broadcasted_iota(jnp.int32, x.shape, 1) // HEAD_DIM
    return jnp.concatenate([jnp.where(head == h, x, jnp.zeros_like(x)) for h in range(4)], axis=0)


def _bd16(x):
    return _block_diag4(x).astype(BF16)


def _rwkv_prep_body(p_ref, hp_ref, hn_ref, mu_ref, w0_ref, w2_ref, a0_ref, a2_ref, g2_ref, kk_ref, ka_ref, rk_ref,
                    ones_ref, *out_refs, ts):
    i = pl.program_id(1)
    n = pl.num_programs(1)
    edge_prev = jnp.where(i > 0, hp_ref[0, 7:8, :], 0.0)
    edge_next = jnp.where(i < n - 1, hn_ref[0, 0:1, :], 0.0)
    consts = (mu_ref, w0_ref, w2_ref, a0_ref, a2_ref, g2_ref, kk_ref, ka_ref, rk_ref, ones_ref)
    chain_stage = None
    for r0 in range(0, ts, PREP_SUB):
        prev_row = edge_prev if r0 == 0 else p_ref[0, r0 - 1:r0, :]
        next_row = edge_next if r0 + PREP_SUB == ts else p_ref[0, r0 + PREP_SUB:r0 + PREP_SUB + 1, :]
        state = {}
        vector_stage = _prep_vector_stage(p_ref, prev_row, next_row, consts, out_refs, r0, PREP_SUB, state)
        _alternate(chain_stage, vector_stage)
        chain_stage = _prep_chain_stage(state, out_refs, r0, PREP_SUB)
    _alternate(chain_stage, None)


def _alternate(first, second):
    live = [g for g in (first, second) if g is not None]
    while live:
        for g in list(live):
            try:
                next(g)
            except StopIteration:
                live.remove(g)


def _prep_vector_stage(p_ref, prev_row, next_row, consts, out_refs, r0, ts, state):
    mu_ref, w0_ref, w2_ref, a0_ref, a2_ref, g2_ref, kk_ref, ka_ref, rk_ref, ones_ref = consts
    dir_refs = (out_refs[0:8], out_refs[8:16])
    v_ref, bonus_ref, gate_ref = out_refs[16:19]
    rows = slice(r0, r0 + ts)

    def shifted_mix(lo, hi):
        p = p_ref[0, rows, lo:hi]
        row = lax.broadcasted_iota(jnp.int32, p.shape, 0)
        prev = jnp.where(row == 0, prev_row[:, lo:hi], pltpu.roll(p, 1, 0))
        nxt = jnp.where(row == ts - 1, next_row[:, lo:hi], pltpu.roll(p, ts - 1, 0))
        return p + mu_ref[:, lo:hi] * (0.5 * (prev + nxt) - p)

    lora = shifted_mix(1536, 1920)
    wl, al, gl = lora[:, 0:128], lora[:, 128:256], lora[:, 256:384]
    yield
    r = shifted_mix(0, 512)
    yield
    k = shifted_mix(512, 1024)
    yield
    v = shifted_mix(1024, 1536)
    yield
    ones = ones_ref[...]

    u = w0_ref[...] + _dot(jnp.tanh(wl).astype(BF16), w2_ref[...])
    logw = (-float(np.exp(-0.5))) * jax.nn.sigmoid(u)
    asig = jax.nn.sigmoid(a0_ref[...] + _dot(al.astype(BF16), a2_ref[...]))
    yield
    kkr = k * kk_ref[...]
    kk = kkr * lax.rsqrt(jnp.maximum(_head_sum(kkr * kkr, ones, 1), 1e-24))
    k_a = ka_ref[...]
    kd = [k * (1.0 + (asig[:, d * 512:(d + 1) * 512] - 1.0) * k_a) for d in range(2)]
    yield
    bonus_ref[0, rows] = (_head_sum(r * rk_ref[...] * (kd[0] + kd[1]), ones, 1) * v).astype(BF16)
    gate_ref[0, rows] = _dot(jax.nn.sigmoid(gl).astype(BF16), g2_ref[...]).astype(BF16)
    v_ref[0, rows] = v.astype(BF16)
    yield

    rr = lax.broadcasted_iota(jnp.int32, (ts, ts), 0)
    cc = lax.broadcasted_iota(jnp.int32, (ts, ts), 1)
    same = (rr // CHUNK) == (cc // CHUNK)

    scaled = []
    for d in range(2):
        bp_ref, kp_ref, pt_ref = dir_refs[d][5:8]
        rt_ref = dir_refs[d][1]
        before = (cc <= rr) if d == 0 else (cc >= rr)
        lhs = jnp.where(same & before, 1.0, 0.0).astype(BF16)
        lw = logw[:, d * 512:(d + 1) * 512]
        cl = None
        for term in _split_bf16(lw, 2):
            y = _dot(lhs, term)
            cl = y if cl is None else cl + y
        yield
        ends = [c * CHUNK + (CHUNK - 1 if d == 0 else 0) for c in range(ts // CHUNK)]
        p_tot = [jnp.exp(cl[e:e + 1, :]) for e in ends]
        for c in range(ts // CHUNK):
            pt_ref[0, r0 // CHUNK + c] = p_tot[c]
        e_neg = jnp.exp(-cl)
        b_raw = kk * asig[:, d * 512:(d + 1) * 512]
        r_t = r * jnp.exp(cl)
        rt_ref[0, rows] = r_t.astype(BF16)
        yield
        e_out = e_neg * jnp.concatenate([jnp.broadcast_to(pc, (CHUNK, RWKV_DIM)) for pc in p_tot], axis=0)
        bp_ref[0, rows] = (b_raw * e_out).astype(BF16)
        kp_ref[0, rows] = (kd[d] * e_out).astype(BF16)
        yield
        scaled.append((-kk * jnp.exp(cl - lw), b_raw * e_neg, kd[d] * e_neg, r_t))
        yield
    state.update(scaled=scaled, v=v)


def _prep_chain_stage(state, out_refs, r0, ts):
    scaled, v = state["scaled"], state["v"]
    dir_refs = (out_refs[0:8], out_refs[8:16])
    tt = lax.broadcasted_iota(jnp.int32, (CHUNK, QUAD), 0)
    ss = lax.broadcasted_iota(jnp.int32, (CHUNK, QUAD), 1) % CHUNK
    eye = jnp.where(tt == ss, 1.0, 0.0)
    chains = [(d, c, qd) for d in range(2) for c in range(ts // CHUNK) for qd in range(RWKV_DIM // QUAD)]
    cut = lambda arr, c, qd: arr[c * CHUNK:(c + 1) * CHUNK, qd * QUAD:(qd + 1) * QUAD]
    strict = [ss < tt, ss > tt]
    incl = [ss <= tt, ss >= tt]
    at, x2 = {}, {}
    for ch in chains:
        d, c, qd = ch
        at[ch] = cut(scaled[d][0], c, qd)
        x2[ch] = jnp.concatenate([at[ch], cut(scaled[d][3], c, qd)], axis=0).astype(BF16)
    w_b = {ch: _dot_nt(x2[ch], _bd16(cut(scaled[ch[0]][1], ch[1], ch[2]))) for ch in chains}
    yield
    w_k = {ch: _dot_nt(x2[ch], _bd16(cut(scaled[ch[0]][2], ch[1], ch[2]))) for ch in chains}
    yield
    a_ab, gy = {}, {}
    for ch in chains:
        d, c, qd = ch
        a_ab[ch] = jnp.where(strict[d], w_b[ch][:CHUNK], 0.0)
        dir_refs[d][2][0, r0 + c * CHUNK:r0 + (c + 1) * CHUNK, qd * QUAD:(qd + 1) * QUAD] = (
            jnp.where(incl[d], w_b[ch][CHUNK:], 0.0).astype(BF16))
        lk = jnp.concatenate([jnp.where(strict[d], w_k[ch][:CHUNK], 0.0),
                              jnp.where(incl[d], w_k[ch][CHUNK:], 0.0)], axis=0)
        gy[ch] = _dot(lk.astype(BF16), _bd16(cut(v, c, qd)))
    yield
    tm = {ch: eye + a_ab[ch] for ch in chains}
    pw = {ch: _dot(a_ab[ch].astype(BF16), _bd16(a_ab[ch])) for ch in chains}
    yield
    for _ in range(4):
        both = {ch: _dot(jnp.concatenate([pw[ch], tm[ch]], axis=0).astype(BF16), _bd16(pw[ch])) for ch in chains}
        tm = {ch: tm[ch] + both[ch][CHUNK:] for ch in chains}
        pw = {ch: both[ch][:CHUNK] for ch in chains}
        yield
    tm = {ch: (tm[ch] + _dot(tm[ch].astype(BF16), _bd16(pw[ch]))).astype(BF16) for ch in chains}
    yield
    for ch in chains:
        d, c, qd = ch
        rs, ls = slice(r0 + c * CHUNK, r0 + (c + 1) * CHUNK), slice(qd * QUAD, (qd + 1) * QUAD)
        at_ref, _, _, uv_ref, yv_ref = dir_refs[d][0:5]
        uv_ref[0, rs, ls] = _dot(tm[ch], _bd16(gy[ch][:CHUNK])).astype(BF16)
        at_ref[0, rs, ls] = _dot(tm[ch], _bd16(at[ch])).astype(BF16)
        yv_ref[0, rs, ls] = gy[ch][CHUNK:].astype(BF16)


def _rwkv_prep(p, mu, w0, w2bd, a0, a2bd, g2, k_k, k_a, r_k, ones_bd):
    b, s, _ = p.shape
    ts = min(PREP_TILE, s)
    nt = s // ts
    h8 = ts // 8
    main = pl.BlockSpec((1, ts, RWKV_COLS), lambda bi, i: (bi, i, 0))
    halo_prev = pl.BlockSpec((1, 8, RWKV_COLS), lambda bi, i: (bi, jnp.maximum(i * h8 - 1, 0), 0))
    halo_next = pl.BlockSpec((1, 8, RWKV_COLS), lambda bi, i: (bi, jnp.minimum((i + 1) * h8, s // 8 - 1), 0))
    tok = pl.BlockSpec((1, ts, RWKV_DIM), lambda bi, i: (bi, i, 0))
    ptot = pl.BlockSpec((1, ts // CHUNK, 1, RWKV_DIM), lambda bi, i: (bi, i, 0, 0))
    tok16 = jax.ShapeDtypeStruct((b, s, RWKV_DIM), BF16)
    tok32 = jax.ShapeDtypeStruct((b, s, RWKV_DIM), F32)
    pt32 = jax.ShapeDtypeStruct((b, s // CHUNK, 1, RWKV_DIM), F32)
    per_dir_specs = [tok, tok, tok, tok, tok, tok, tok, ptot]
    per_dir_shapes = [tok16] * 7 + [pt32]
    consts = [mu, w0, w2bd, a0, a2bd, g2, k_k, k_a, r_k, ones_bd]
    return pl.pallas_call(
        functools.partial(_rwkv_prep_body, ts=ts),
        grid=(b, nt),
        in_specs=[main, halo_prev, halo_next] + [_resident(c.shape) for c in consts],
        out_specs=per_dir_specs * 2 + [tok, tok, tok],
        out_shape=per_dir_shapes * 2 + [tok16, tok16, tok16],
        compiler_params=_params("parallel", "parallel"),
        name="rwkv_prep",
    )(p, p, p, *consts)


def _rwkv_scan_body(*refs, nc, nb):
    fwd, bwd = refs[0:8], refs[8:16]
    vf_ref, vb_ref = refs[16:18]
    yf_ref, yb_ref = refs[18:20]
    st_ref = refs[20]
    step = pl.program_id(1)

    @pl.when(step == 0)
    def _():
        st_ref[...] = jnp.zeros_like(st_ref)

    rr = lax.broadcasted_iota(jnp.int32, (QUAD, QUAD), 0) // HEAD_DIM
    cc = lax.broadcasted_iota(jnp.int32, (QUAD, QUAD), 1) // HEAD_DIM
    diag = rr == cc
    dirs = ((fwd, vf_ref, yf_ref), (bwd, vb_ref, yb_ref))
    chains = [(bi, d, qd) for bi in range(nb) for d in range(2) for qd in range(RWKV_DIM // QUAD)]
    st = {ch: st_ref[ch] for ch in chains}
    for j in range(nc):
        z = {}
        for ch in chains:
            bi, d, qd = ch
            c = j if d == 0 else nc - 1 - j
            rs, ls = slice(c * CHUNK, (c + 1) * CHUNK), slice(qd * QUAD, (qd + 1) * QUAD)
            at_ref, rt_ref = dirs[d][0][0:2]
            x2 = jnp.concatenate([at_ref[bi, rs, ls], rt_ref[bi, rs, ls]], axis=0)
            z[ch] = _dot_nt(x2, st[ch].astype(BF16))
        for ch in chains:
            bi, d, qd = ch
            c = j if d == 0 else nc - 1 - j
            rs, ls = slice(c * CHUNK, (c + 1) * CHUNK), slice(qd * QUAD, (qd + 1) * QUAD)
            (_, _, mrb_ref, uv_ref, yv_ref, bp_ref, kp_ref, pt_ref), v_ref, y_ref = dirs[d]
            u = z[ch][:CHUNK] + uv_ref[bi, rs, ls]
            y = z[ch][CHUNK:] + _dot(mrb_ref[bi, rs, ls], _bd16(u)) + yv_ref[bi, rs, ls]
            y_ref[bi, rs, ls] = y.astype(BF16)
            uv2 = jnp.concatenate([u.astype(BF16), v_ref[bi, rs, ls]], axis=0)
            bk2 = jnp.concatenate([bp_ref[bi, rs, ls], kp_ref[bi, rs, ls]], axis=0)
            st[ch] = st[ch] * pt_ref[bi, c][:, ls] + jnp.where(diag, _dot_tn(uv2, bk2), 0.0)
    for ch in chains:
        st_ref[ch] = st[ch]


def _rwkv_scan(fwd_arrs, bwd_arrs, v16):
    b, s, _ = v16.shape
    nc = min(SCAN_CHUNKS, s // CHUNK)
    nb = SCAN_BATCH if b % SCAN_BATCH == 0 else 1
    rows = nc * CHUNK
    steps = s // rows
    f_tok = pl.BlockSpec((nb, rows, RWKV_DIM), lambda bi, i: (bi, i, 0))
    b_tok = pl.BlockSpec((nb, rows, RWKV_DIM), lambda bi, i: (bi, steps - 1 - i, 0))
    f_pt = pl.BlockSpec((nb, nc, 1, RWKV_DIM), lambda bi, i: (bi, i, 0, 0))
    b_pt = pl.BlockSpec((nb, nc, 1, RWKV_DIM), lambda bi, i: (bi, steps - 1 - i, 0, 0))
    y_shape = jax.ShapeDtypeStruct((b, s, RWKV_DIM), BF16)
    return pl.pallas_call(
        functools.partial(_rwkv_scan_body, nc=nc, nb=nb),
        grid=(b // nb, steps),
        in_specs=[f_tok] * 7 + [f_pt] + [b_tok] * 7 + [b_pt] + [f_tok, b_tok],
        out_specs=[f_tok, b_tok],
        out_shape=[y_shape, y_shape],
        scratch_shapes=[pltpu.VMEM((nb, 2, RWKV_DIM // QUAD, QUAD, QUAD), F32)],
        compiler_params=_params("parallel", "arbitrary"),
        name="rwkv_scan",
    )(*fwd_arrs, *bwd_arrs, v16, v16)


N_PAIRS = 4


def _pair_queries(q_ref, pair, rows=slice(None)):
    qp = q_ref[0, rows, pair * LANES:(pair + 1) * LANES]
    lane = lax.broadcasted_iota(jnp.int32, qp.shape, 1)
    zero = jnp.zeros_like(qp)
    return jnp.concatenate([jnp.where(lane < HEAD_DIM, qp, zero), jnp.where(lane >= HEAD_DIM, qp, zero)], axis=0)


def _store_pair(o_ref, pair, o, n, rows=slice(None)):
    lane = lax.broadcasted_iota(jnp.int32, (n, LANES), 1)
    o_ref[0, rows, pair * LANES:(pair + 1) * LANES] = jnp.where(lane < HEAD_DIM, o[:n], o[n:]).astype(BF16)


def _values_and_ones(vd):
    return jnp.concatenate([vd, jnp.ones_like(vd)], axis=1)


def _skewed_pairs(scores, softmax, finish, pieces=tuple(range(N_PAIRS))):
    sc, pr = {}, {}
    n = len(pieces)
    for step in range(n + 2):
        if step < n:
            sc[step] = scores(pieces[step])
        if 1 <= step <= n:
            pr[step - 1] = softmax(pieces[step - 1], sc.pop(step - 1))
        if step >= 2:
            finish(pieces[step - 2], pr.pop(step - 2))


def _ga_body(q_ref, kv_ref, o_ref, *, tq):
    def scores(p):
        return _dot_nt(_pair_queries(q_ref, p), kv_ref[0, :, (p // 2) * LANES:(p // 2 + 1) * LANES])

    def softmax(p, sc):
        return jnp.exp2(sc - jnp.max(sc, axis=-1, keepdims=True)).astype(BF16)

    def finish(p, e):
        ov = _dot(e, _values_and_ones(kv_ref[0, :, (2 + p // 2) * LANES:(3 + p // 2) * LANES]))
        _store_pair(o_ref, p, ov[:, :LANES] / ov[:, LANES:], tq)

    _skewed_pairs(scores, softmax, finish)


def _global_attention(q, kv):
    b, s, w = q.shape
    tq = min(GA_Q_TILE, s)
    qs = pl.BlockSpec((1, tq, w), lambda bi, i: (bi, i, 0))
    kvs = pl.BlockSpec((1, s, w), lambda bi, i: (bi, 0, 0))
    return pl.pallas_call(
        functools.partial(_ga_body, tq=tq),
        grid=(b, s // tq),
        in_specs=[qs, kvs],
        out_specs=qs,
        out_shape=jax.ShapeDtypeStruct((b, s, w), BF16),
        compiler_params=_params("parallel", "arbitrary"),
        name="global_attn",
    )(q, kv)


def _wa_body(sink_ref, q_ref, kv_ref, *refs, blk, span, seq, nsub):
    band_refs, o_ref = refs[:nsub], refs[nsub]
    first = pl.program_id(1) * nsub
    starts = [pl.multiple_of(jnp.clip((first + j) * blk - WINDOW, 0, seq - span), WINDOW) for j in range(nsub)]
    valid = []
    for j in range(nsub):
        band = band_refs[j][0] > 0.5
        valid.append(jnp.concatenate([band, band], axis=0))

    def scores(piece):
        p, j = piece
        kd = kv_ref[0, pl.ds(starts[j], span), (p // 2) * LANES:(p // 2 + 1) * LANES]
        sc = _dot_nt(_pair_queries(q_ref, p, slice(j * blk, (j + 1) * blk)), kd)
        return jnp.where(valid[j], sc, NEG_INF)

    def softmax(piece, sc):
        p, _ = piece
        e, sink_e = [], []
        for h in range(2):
            sc_h = sc[h * blk:(h + 1) * blk]
            sink = sink_ref[2 * p + h] * LOG2E
            m = jnp.maximum(jnp.max(sc_h, axis=-1, keepdims=True), sink)
            e.append(jnp.exp2(sc_h - m).astype(BF16))
            sink_e.append(jnp.exp2(sink - m))
        return jnp.concatenate(e, axis=0), jnp.concatenate(sink_e, axis=0)

    def finish(piece, es):
        p, j = piece
        e, sink_e = es
        vd = kv_ref[0, pl.ds(starts[j], span), (2 + p // 2) * LANES:(3 + p // 2) * LANES]
        ov = _dot(e, _values_and_ones(vd))
        _store_pair(o_ref, p, ov[:, :LANES] / (ov[:, LANES:] + sink_e), blk, slice(j * blk, (j + 1) * blk))

    _skewed_pairs(scores, softmax, finish, [(p, j) for p in range(N_PAIRS) for j in range(nsub)])


def _window_attention(q, kv, sink):
    b, s, w = q.shape
    blk = min(WA_Q_BLOCK, s)
    span = min(blk + 2 * WINDOW, s)
    nblk = s // blk
    nsub = 2 if nblk % 2 == 0 else 1
    offsets = sorted({int(np.clip(i * blk - WINDOW, 0, s - span)) - i * blk for i in range(nblk)}, reverse=True)
    rel = jnp.arange(span)[None, :] - jnp.arange(blk)[:, None]
    band = jnp.stack([(jnp.abs(rel + off) <= WINDOW).astype(F32) for off in offsets])
    which = lambda i: jnp.where(i == 0, 0, jnp.where(i == nblk - 1, len(offsets) - 1, min(1, len(offsets) - 1)))
    qs = pl.BlockSpec((1, nsub * blk, w), lambda bi, i, *_: (bi, i, 0))
    kvs = pl.BlockSpec((1, s, w), lambda bi, i, *_: (bi, 0, 0))
    bands = [pl.BlockSpec((1, blk, span), functools.partial(lambda j, bi, i, *_: (which(i * nsub + j), 0, 0), j))
             for j in range(nsub)]
    return pl.pallas_call(
        functools.partial(_wa_body, blk=blk, span=span, seq=s, nsub=nsub),
        grid_spec=pltpu.PrefetchScalarGridSpec(
            num_scalar_prefetch=1, grid=(b, nblk // nsub), in_specs=[qs, kvs] + bands, out_specs=qs),
        out_shape=jax.ShapeDtypeStruct((b, s, w), BF16),
        compiler_params=_params("parallel", "arbitrary"),
        name="window_attn",
    )(sink, q, kv, *([band] * nsub))


def _merge_body(x_ref, g_ref, wg_ref, wf_ref, wb_ref, bonus_ref, rgate_ref, lg_ref, lb_ref, ones_ref,
                yb_ref, yc_ref, woa_ref, wob_ref, woc_ref, wout_ref, o_ref):
    ones = ones_ref[...]

    def prepare(rs):
        h = _rmsnorm(x_ref[rs, :], g_ref[...]).astype(BF16)
        wkv = wf_ref[rs, :].astype(F32) + wb_ref[rs, :].astype(F32)
        dev = wkv - _head_sum(wkv, ones, 2) * (1.0 / HEAD_DIM)
        var = _head_sum(dev * dev, ones, 1) * (1.0 / HEAD_DIM)
        y_a = dev * lax.rsqrt(var + LNX_EPS) * lg_ref[...] + lb_ref[...] + bonus_ref[rs, :].astype(F32)
        return h, (y_a * rgate_ref[rs, :].astype(F32)).astype(BF16)

    def gated_sum(rs, h, y_a):
        gates = jax.nn.sigmoid(_dot(h, wg_ref[...]))
        merged = None
        for j, (y, wo_ref) in enumerate(((y_a, woa_ref), (yb_ref[rs, :], wob_ref), (yc_ref[rs, :], woc_ref))):
            term = gates[:, j * D_MODEL:(j + 1) * D_MODEL] * _dot(y, wo_ref[...])
            merged = term if merged is None else merged + term
        return merged.astype(BF16)

    parts = _row_halves(x_ref.shape[0])
    prepared = [prepare(rs) for rs in parts]
    merged = [gated_sum(rs, *pr) for rs, pr in zip(parts, prepared)]
    for rs, m in zip(parts, merged):
        o_ref[rs, :] = x_ref[rs, :] + _dot(m, wout_ref[...])


def _merge(x, g, wg, wkv_f, wkv_b, bonus, rgate, lnx_g, lnx_b, ones_bd, yb, yc, woa, wob, woc, wout):
    t, d = x.shape
    tm = min(TOKEN_TILE, t)
    row = lambda w_: pl.BlockSpec((tm, w_), lambda i: (i, 0))
    half = row(RWKV_DIM)
    return pl.pallas_call(
        _merge_body,
        grid=(t // tm,),
        in_specs=[row(d), _resident((1, d)), _resident(wg.shape), half, half, half, half,
                  _resident(lnx_g.shape), _resident(lnx_b.shape), _resident(ones_bd.shape), half, half,
                  _resident(woa.shape), _resident(wob.shape), _resident(woc.shape), _resident(wout.shape)],
        out_specs=row(d),
        out_shape=jax.ShapeDtypeStruct((t, d), F32),
        compiler_params=_params("parallel"),
        name="merge",
    )(x, g, wg, wkv_f, wkv_b, bonus, rgate, lnx_g, lnx_b, ones_bd, yb, yc, woa, wob, woc, wout)


def _rope_tables(seq):
    t = jnp.arange(seq)

    def angles(pos, dim):
        inv_freq = ROPE_THETA ** (-jnp.arange(0, dim, 2, dtype=F32) / dim)
        return pos.astype(F32)[:, None] * inv_freq[None, :]

    ang_row, ang_col = angles(t // GRID_W, HEAD_DIM // 2), angles(t % GRID_W, HEAD_DIM // 2)
    ang_1d = angles(t, HEAD_DIM)
    ga_cos = jnp.concatenate([jnp.cos(ang_row)] * 2 + [jnp.cos(ang_col)] * 2, axis=1)
    ga_sin = jnp.concatenate([-jnp.sin(ang_row), jnp.sin(ang_row), -jnp.sin(ang_col), jnp.sin(ang_col)], axis=1)
    wa_cos = jnp.concatenate([jnp.cos(ang_1d)] * 2, axis=1)
    wa_sin = jnp.concatenate([-jnp.sin(ang_1d), jnp.sin(ang_1d)], axis=1)
    return tuple(jnp.concatenate([a, a], axis=1) for a in (ga_cos, ga_sin, wa_cos, wa_sin))


def _two_way(w):
    z = jnp.zeros_like(w[0])
    return jnp.concatenate([jnp.concatenate([w[0], z], axis=1), jnp.concatenate([z, w[1]], axis=1)], axis=0)


def kernel(x, ffn1_norm, ffn1_w_up, ffn1_w_down, mix_norm, w_in, rwkv_mu, rwkv_w0, rwkv_w2, rwkv_a0, rwkv_a2, rwkv_g2, rwkv_k_k, rwkv_k_a, rwkv_r_k, rwkv_lnx_g, rwkv_lnx_b, ga_q_norm, ga_k_norm, wa_sink, w_o_rwkv, w_o_ga, w_o_wa, w_out, ffn2_norm, ffn2_w_up, ffn2_w_down, final_norm):
    bsz, seq, dm = x.shape
    depth = w_in.shape[0]
    tokens = bsz * seq
    tabs = _rope_tables(seq)
    idx = jnp.arange(512)
    ones_bd = (idx[:, None] // HEAD_DIM == idx[None, :] // HEAD_DIM).astype(BF16)
    row1 = lambda a: a.reshape(1, -1)
    final_g = row1(final_norm)

    xt = x.reshape(tokens, dm)
    for l in range(depth):
        xt = _ffn(xt, row1(ffn1_norm[l]), ffn1_w_up[l].astype(BF16), ffn1_w_down[l].astype(BF16), final_g, False)

        w_mix = w_in[l, :, :MIX_COLS].astype(BF16)
        w_gate = w_in[l, :, MIX_COLS:].astype(BF16)
        p_rwkv, ga_q, ga_kv, wa_q, wa_kv = _inproj(
            xt, row1(mix_norm[l]), w_mix, ones_bd, row1(jnp.tile(ga_q_norm[l], 8)), row1(jnp.tile(ga_k_norm[l], 8)),
            tabs, seq)

        outs = _rwkv_prep(
            p_rwkv.reshape(bsz, seq, RWKV_COLS), row1(rwkv_mu[l]), row1(rwkv_w0[l]), _two_way(rwkv_w2[l]).astype(BF16),
            row1(rwkv_a0[l]), _two_way(rwkv_a2[l]).astype(BF16), rwkv_g2[l].astype(BF16), row1(rwkv_k_k[l]),
            row1(rwkv_k_a[l]), row1(rwkv_r_k[l]), ones_bd)
        v16, bonus, gate = outs[16:19]
        y_f, y_b = _rwkv_scan(outs[0:8], outs[8:16], v16)
        flat = lambda a: a.reshape(tokens, RWKV_DIM)

        y_g = _global_attention(ga_q.reshape(bsz, seq, 512), ga_kv.reshape(bsz, seq, 512))
        y_w = _window_attention(wa_q.reshape(bsz, seq, 512), wa_kv.reshape(bsz, seq, 512), wa_sink[l])

        xt = _merge(xt, row1(mix_norm[l]), w_gate, flat(y_f), flat(y_b), flat(bonus), flat(gate),
                    row1(rwkv_lnx_g[l]), row1(rwkv_lnx_b[l]), ones_bd, flat(y_g), flat(y_w),
                    w_o_rwkv[l].astype(BF16), w_o_ga[l].astype(BF16), w_o_wa[l].astype(BF16), w_out[l].astype(BF16))

        xt = _ffn(xt, row1(ffn2_norm[l]), ffn2_w_up[l].astype(BF16), ffn2_w_down[l].astype(BF16), final_g,
                  l == depth - 1)
    return xt.reshape(bsz, seq, dm)
```

```python
import functools

import jax
import jax.numpy as jnp
import numpy as np
from jax import lax
from jax.experimental import pallas as pl
from jax.experimental.pallas import tpu as pltpu

F32 = jnp.float32
BF16 = jnp.bfloat16

D_MODEL = 1024
HEAD_DIM = 64
RWKV_DIM = 512
RWKV_COLS = 1920
ATT_Q = 8 * HEAD_DIM
ATT_KV = 2 * HEAD_DIM
GA_COLS = ATT_Q + 2 * ATT_KV
WA_COLS = ATT_Q + 2 * ATT_KV
MIX_COLS = RWKV_COLS + GA_COLS + WA_COLS
D_FF = 2816
GRID_W = 64
WINDOW = 128
ROPE_THETA = 10000.0
NORM_EPS = 1e-6
LNX_EPS = HEAD_DIM * 1e-5
NEG_INF = -1e30
LOG2E = float(np.log2(np.e))
ATTN_SCALE_LOG2 = HEAD_DIM ** -0.5 * LOG2E

CHUNK = 64
QUAD = 4 * HEAD_DIM
LANES = 128
VMEM_LIMIT = 56 * 1024 * 1024

TOKEN_TILE = 1024
ROW_PART = 256
INPROJ_PART = 128
FF_CHUNK = 2816
PREP_TILE = 512
PREP_SUB = 128
SCAN_CHUNKS = 4
SCAN_BATCH = 2
GA_Q_BLOCK = 256
GA_Q_TILE = 512
WA_Q_BLOCK = 128


def _params(*sem):
    return pltpu.CompilerParams(dimension_semantics=sem, vmem_limit_bytes=VMEM_LIMIT)


def _resident(shape):
    nd = len(shape)
    return pl.BlockSpec(shape, lambda *_: (0,) * nd, pipeline_mode=pl.Buffered(1))


def _rmsnorm(x, g):
    return x * lax.rsqrt(jnp.mean(x * x, axis=-1, keepdims=True) + NORM_EPS) * g


def _row_parts(rows, part=None):
    part = min(part or ROW_PART, rows // 2) if rows % 32 == 0 else rows
    return [slice(r, r + part) for r in range(0, rows, part)]


def _dot(a, b):
    return jnp.dot(a, b, preferred_element_type=F32)


def _dot_nt(a, b):
    return lax.dot_general(a, b, (((1,), (1,)), ((), ())), preferred_element_type=F32)


def _dot_tn(a, b):
    return lax.dot_general(a, b, (((0,), (0,)), ((), ())), preferred_element_type=F32)


def _split_bf16(x, terms):
    out = []
    for _ in range(terms - 1):
        hi = x.astype(BF16)
        out.append(hi)
        x = x - hi.astype(F32)
    out.append(x.astype(BF16))
    return out


def _head_sum(x, ones_bd, terms):
    w = x.shape[-1]
    blk = min(w, QUAD)
    ones = ones_bd[:blk, :blk]
    parts = []
    for c0 in range(0, w, blk):
        acc = None
        for t in _split_bf16(x[:, c0:c0 + blk], terms):
            y = _dot(t, ones)
            acc = y if acc is None else acc + y
        parts.append(acc)
    return parts[0] if len(parts) == 1 else jnp.concatenate(parts, axis=1)


def _ffn_body(x_ref, g_ref, wup_ref, wdn_ref, gf_ref, o_ref, act_ref, *, final_norm):
    halves = _row_parts(x_ref.shape[0])
    h = [_rmsnorm(x_ref[rs, :], g_ref[...]).astype(BF16) for rs in halves]
    for c in range(D_FF // FF_CHUNK):
        lo, hi = c * FF_CHUNK, (c + 1) * FF_CHUNK
        for rs, hh in zip(halves, h):
            gate = _dot(hh, wup_ref[:, lo:hi])
            up = _dot(hh, wup_ref[:, D_FF + lo:D_FF + hi])
            act_ref[rs, lo:hi] = (gate * jax.nn.sigmoid(gate) * up).astype(BF16)
    for rs in halves:
        y = x_ref[rs, :] + 0.5 * _dot(act_ref[rs, :], wdn_ref[...])
        if final_norm:
            y = _rmsnorm(y, gf_ref[...])
        o_ref[rs, :] = y


def _ffn(x, g, w_up, w_down, g_final, final_norm):
    t, d = x.shape
    tm = min(TOKEN_TILE, t)
    row = pl.BlockSpec((tm, d), lambda i: (i, 0))
    return pl.pallas_call(
        functools.partial(_ffn_body, final_norm=final_norm),
        grid=(t // tm,),
        in_specs=[row, _resident((1, d)), _resident(w_up.shape), _resident(w_down.shape), _resident((1, d))],
        out_specs=row,
        out_shape=jax.ShapeDtypeStruct((t, d), F32),
        scratch_shapes=[pltpu.VMEM((tm, D_FF), BF16)],
        compiler_params=_params("parallel"),
        name="ffn_final" if final_norm else "ffn",
    )(x, g, w_up, w_down, g_final)


def _rope(t, cos, sin_signed, half):
    w = t.shape[-1]
    lane = lax.broadcasted_iota(jnp.int32, t.shape, 1)
    first = (lane % (2 * half)) < half
    partner = jnp.where(first, pltpu.roll(t, w - half, 1), pltpu.roll(t, half, 1))
    return t * cos + partner * sin_signed


def _dup_heads(t):
    lane = lax.broadcasted_iota(jnp.int32, t.shape, 1)
    swapped = pltpu.roll(t, HEAD_DIM, 1)
    low = lane < HEAD_DIM
    return jnp.where(low, t, swapped), jnp.where(low, swapped, t)


def _inproj_body(x_ref, g_ref, w_ref, ones_ref, gq_ref, gk_ref, cga_ref, sga_ref, cwa_ref, swa_ref,
                 prw_ref, gaq_ref, gakv_ref, waq_ref, wakv_ref):
    ones = ones_ref[...]

    def head_norm(t, gain):
        ms = _head_sum(t * t, ones, 1) * (1.0 / HEAD_DIM)
        return t * lax.rsqrt(ms + NORM_EPS) * gain

    tile4 = lambda tab: jnp.concatenate([tab] * 4, axis=1)

    def project(rs):
        h = _rmsnorm(x_ref[rs, :], g_ref[...]).astype(BF16)
        prw_ref[rs, :] = _dot(h, w_ref[:, :RWKV_COLS])
        return _dot(h, w_ref[:, RWKV_COLS:RWKV_COLS + GA_COLS]), _dot(h, w_ref[:, RWKV_COLS + GA_COLS:MIX_COLS])

    def finish(rs, ga, wa):
        cga, sga = cga_ref[rs, :], sga_ref[rs, :]
        q = _rope(head_norm(ga[:, :ATT_Q], gq_ref[...]), tile4(cga), tile4(sga), 16)
        k = _rope(head_norm(ga[:, ATT_Q:ATT_Q + ATT_KV], gk_ref[:, :ATT_KV]), cga, sga, 16)
        gaq_ref[rs, :] = (q * ATTN_SCALE_LOG2).astype(BF16)
        k0, k1 = _dup_heads(k)
        v0, v1 = _dup_heads(ga[:, ATT_Q + ATT_KV:])
        gakv_ref[rs, :] = jnp.concatenate([k0, k1, v0, v1], axis=1).astype(BF16)
        cwa, swa = cwa_ref[rs, :], swa_ref[rs, :]
        q = _rope(wa[:, :ATT_Q], tile4(cwa), tile4(swa), 32)
        k = _rope(wa[:, ATT_Q:ATT_Q + ATT_KV], cwa, swa, 32)
        waq_ref[rs, :] = (q * ATTN_SCALE_LOG2).astype(BF16)
        k0, k1 = _dup_heads(k)
        v0, v1 = _dup_heads(wa[:, ATT_Q + ATT_KV:])
        wakv_ref[rs, :] = jnp.concatenate([k0, k1, v0, v1], axis=1).astype(BF16)

    parts = _row_parts(x_ref.shape[0], INPROJ_PART)
    projected = {}
    for j in range(len(parts) + 1):
        if j < len(parts):
            projected[j] = project(parts[j])
        if j >= 1:
            finish(parts[j - 1], *projected.pop(j - 1))


def _inproj(x, g, w, ones_bd, gq, gk, tabs, seq):
    t, d = x.shape
    tm = min(TOKEN_TILE, seq)
    per_seq = seq // tm
    row = lambda w_: pl.BlockSpec((tm, w_), lambda i: (i, 0))
    tab = pl.BlockSpec((tm, LANES), lambda i: (i % per_seq, 0))
    return pl.pallas_call(
        _inproj_body,
        grid=(t // tm,),
        in_specs=[row(d), _resident((1, d)), _resident(w.shape), _resident(ones_bd.shape),
                  _resident(gq.shape), _resident(gk.shape), tab, tab, tab, tab],
        out_specs=[row(RWKV_COLS)] + [row(ATT_Q)] * 4,
        out_shape=[jax.ShapeDtypeStruct((t, RWKV_COLS), F32)] + [jax.ShapeDtypeStruct((t, ATT_Q), BF16)] * 4,
        compiler_params=_params("parallel"),
        name="inproj",
    )(x, g, w, ones_bd, gq, gk, *tabs)


def _block_diag4(x):
    head = lax.broadcasted_iota(jnp.int32, x.shape, 1) // HEAD_DIM
    return jnp.concatenate([jnp.where(head == h, x, jnp.zeros_like(x)) for h in range(4)], axis=0)


def _bd16(x):
    return _block_diag4(x).astype(BF16)


def _rwkv_prep_body(p_ref, hp_ref, hn_ref, mu_ref, w0_ref, w2_ref, a0_ref, a2_ref, g2_ref, kk_ref, ka_ref, rk_ref,
                    ones_ref, *out_refs, ts):
    i = pl.program_id(1)
    n = pl.num_programs(1)
    edge_prev = jnp.where(i > 0, hp_ref[0, 7:8, :], 0.0)
    edge_next = jnp.where(i < n - 1, hn_ref[0, 0:1, :], 0.0)
    consts = (mu_ref, w0_ref, w2_ref, a0_ref, a2_ref, g2_ref, kk_ref, ka_ref, rk_ref, ones_ref)
    chain_stage = None
    for r0 in range(0, ts, PREP_SUB):
        prev_row = edge_prev if r0 == 0 else p_ref[0, r0 - 1:r0, :]
        next_row = edge_next if r0 + PREP_SUB == ts else p_ref[0, r0 + PREP_SUB:r0 + PREP_SUB + 1, :]
        state = {}
        vector_stage = _prep_vector_stage(p_ref, prev_row, next_row, consts, out_refs, r0, PREP_SUB, state)
        _alternate(chain_stage, vector_stage)
        chain_stage = _prep_chain_stage(state, out_refs, r0, PREP_SUB)
    _alternate(chain_stage, None)


def _alternate(first, second):
    live = [g for g in (first, second) if g is not None]
    while live:
        for g in list(live):
            try:
                next(g)
            except StopIteration:
                live.remove(g)


def _prep_vector_stage(p_ref, prev_row, next_row, consts, out_refs, r0, ts, state):
    mu_ref, w0_ref, w2_ref, a0_ref, a2_ref, g2_ref, kk_ref, ka_ref, rk_ref, ones_ref = consts
    dir_refs = (out_refs[0:8], out_refs[8:16])
    v_ref, bonus_ref, gate_ref = out_refs[16:19]
    rows = slice(r0, r0 + ts)

    def shifted_mix(lo, hi):
        p = p_ref[0, rows, lo:hi]
        row = lax.broadcasted_iota(jnp.int32, p.shape, 0)
        prev = jnp.where(row == 0, prev_row[:, lo:hi], pltpu.roll(p, 1, 0))
        nxt = jnp.where(row == ts - 1, next_row[:, lo:hi], pltpu.roll(p, ts - 1, 0))
        return p + mu_ref[:, lo:hi] * (0.5 * (prev + nxt) - p)

    lora = shifted_mix(3 * RWKV_DIM, RWKV_COLS)
    wl, al, gl = lora[:, 0:LANES], lora[:, LANES:2 * LANES], lora[:, 2 * LANES:3 * LANES]
    yield
    r = shifted_mix(0, RWKV_DIM)
    yield
    k = shifted_mix(RWKV_DIM, 2 * RWKV_DIM)
    yield
    v = shifted_mix(2 * RWKV_DIM, 3 * RWKV_DIM)
    yield
    ones = ones_ref[...]

    u = w0_ref[...] + _dot(jnp.tanh(wl).astype(BF16), w2_ref[...])
    logw = (-float(np.exp(-0.5))) * jax.nn.sigmoid(u)
    asig = jax.nn.sigmoid(a0_ref[...] + _dot(al.astype(BF16), a2_ref[...]))
    yield
    kkr = k * kk_ref[...]
    kk = kkr * lax.rsqrt(jnp.maximum(_head_sum(kkr * kkr, ones, 1), 1e-24))
    k_a = ka_ref[...]
    kd = [k * (1.0 + (asig[:, d * RWKV_DIM:(d + 1) * RWKV_DIM] - 1.0) * k_a) for d in range(2)]
    yield
    bonus_ref[0, rows] = (_head_sum(r * rk_ref[...] * (kd[0] + kd[1]), ones, 1) * v).astype(BF16)
    gate_ref[0, rows] = _dot(jax.nn.sigmoid(gl).astype(BF16), g2_ref[...]).astype(BF16)
    v_ref[0, rows] = v.astype(BF16)
    yield

    rr = lax.broadcasted_iota(jnp.int32, (ts, ts), 0)
    cc = lax.broadcasted_iota(jnp.int32, (ts, ts), 1)
    same = (rr // CHUNK) == (cc // CHUNK)

    scaled = []
    for d in range(2):
        bp_ref, kp_ref, pt_ref = dir_refs[d][5:8]
        rt_ref = dir_refs[d][1]
        before = (cc <= rr) if d == 0 else (cc >= rr)
        lhs = jnp.where(same & before, 1.0, 0.0).astype(BF16)
        lw = logw[:, d * RWKV_DIM:(d + 1) * RWKV_DIM]
        cl = None
        for term in _split_bf16(lw, 2):
            y = _dot(lhs, term)
            cl = y if cl is None else cl + y
        yield
        ends = [c * CHUNK + (CHUNK - 1 if d == 0 else 0) for c in range(ts // CHUNK)]
        p_tot = [jnp.exp(cl[e:e + 1, :]) for e in ends]
        for c in range(ts // CHUNK):
            pt_ref[0, r0 // CHUNK + c] = p_tot[c]
        e_neg = jnp.exp(-cl)
        b_raw = kk * asig[:, d * RWKV_DIM:(d + 1) * RWKV_DIM]
        r_t = r * jnp.exp(cl)
        rt_ref[0, rows] = r_t.astype(BF16)
        yield
        e_out = e_neg * jnp.concatenate([jnp.broadcast_to(pc, (CHUNK, RWKV_DIM)) for pc in p_tot], axis=0)
        bp_ref[0, rows] = (b_raw * e_out).astype(BF16)
        kp_ref[0, rows] = (kd[d] * e_out).astype(BF16)
        yield
        scaled.append((-kk * jnp.exp(cl - lw), b_raw * e_neg, kd[d] * e_neg, r_t))
        yield
    state.update(scaled=scaled, v=v)


def _prep_chain_stage(state, out_refs, r0, ts):
    scaled, v = state["scaled"], state["v"]
    dir_refs = (out_refs[0:8], out_refs[8:16])
    tt = lax.broadcasted_iota(jnp.int32, (CHUNK, QUAD), 0)
    ss = lax.broadcasted_iota(jnp.int32, (CHUNK, QUAD), 1) % CHUNK
    eye = jnp.where(tt == ss, 1.0, 0.0)
    chains = [(d, c, qd) for d in range(2) for c in range(ts // CHUNK) for qd in range(RWKV_DIM // QUAD)]
    cut = lambda arr, c, qd: arr[c * CHUNK:(c + 1) * CHUNK, qd * QUAD:(qd + 1) * QUAD]
    strict = [ss < tt, ss > tt]
    incl = [ss <= tt, ss >= tt]
    at, x2 = {}, {}
    for ch in chains:
        d, c, qd = ch
        at[ch] = cut(scaled[d][0], c, qd)
        x2[ch] = jnp.concatenate([at[ch], cut(scaled[d][3], c, qd)], axis=0).astype(BF16)
    w_b = {ch: _dot_nt(x2[ch], _bd16(cut(scaled[ch[0]][1], ch[1], ch[2]))) for ch in chains}
    yield
    w_k = {ch: _dot_nt(x2[ch], _bd16(cut(scaled[ch[0]][2], ch[1], ch[2]))) for ch in chains}
    yield
    a_ab, gy = {}, {}
    for ch in chains:
        d, c, qd = ch
        a_ab[ch] = jnp.where(strict[d], w_b[ch][:CHUNK], 0.0)
        dir_refs[d][2][0, r0 + c * CHUNK:r0 + (c + 1) * CHUNK, qd * QUAD:(qd + 1) * QUAD] = (
            jnp.where(incl[d], w_b[ch][CHUNK:], 0.0).astype(BF16))
        lk = jnp.concatenate([jnp.where(strict[d], w_k[ch][:CHUNK], 0.0),
                              jnp.where(incl[d], w_k[ch][CHUNK:], 0.0)], axis=0)
        gy[ch] = _dot(lk.astype(BF16), _bd16(cut(v, c, qd)))
    yield
    tm = {ch: eye + a_ab[ch] for ch in chains}
    pw = {ch: _dot(a_ab[ch].astype(BF16), _bd16(a_ab[ch])) for ch in chains}
    yield
    for _ in range(4):
        both = {ch: _dot(jnp.concatenate([pw[ch], tm[ch]], axis=0).astype(BF16), _bd16(pw[ch])) for ch in chains}
        tm = {ch: tm[ch] + both[ch][CHUNK:] for ch in chains}
        pw = {ch: both[ch][:CHUNK] for ch in chains}
        yield
    tm = {ch: (tm[ch] + _dot(tm[ch].astype(BF16), _bd16(pw[ch]))).astype(BF16) for ch in chains}
    yield
    for ch in chains:
        d, c, qd = ch
        rs, ls = slice(r0 + c * CHUNK, r0 + (c + 1) * CHUNK), slice(qd * QUAD, (qd + 1) * QUAD)
        at_ref, _, _, uv_ref, yv_ref = dir_refs[d][0:5]
        uv_ref[0, rs, ls] = _dot(tm[ch], _bd16(gy[ch][:CHUNK])).astype(BF16)
        at_ref[0, rs, ls] = _dot(tm[ch], _bd16(at[ch])).astype(BF16)
        yv_ref[0, rs, ls] = gy[ch][CHUNK:].astype(BF16)


def _rwkv_prep(p, mu, w0, w2bd, a0, a2bd, g2, k_k, k_a, r_k, ones_bd):
    b, s, _ = p.shape
    ts = min(PREP_TILE, s)
    nt = s // ts
    h8 = ts // 8
    main = pl.BlockSpec((1, ts, RWKV_COLS), lambda bi, i: (bi, i, 0))
    halo_prev = pl.BlockSpec((1, 8, RWKV_COLS), lambda bi, i: (bi, jnp.maximum(i * h8 - 1, 0), 0))
    halo_next = pl.BlockSpec((1, 8, RWKV_COLS), lambda bi, i: (bi, jnp.minimum((i + 1) * h8, s // 8 - 1), 0))
    tok = pl.BlockSpec((1, ts, RWKV_DIM), lambda bi, i: (bi, i, 0))
    ptot = pl.BlockSpec((1, ts // CHUNK, 1, RWKV_DIM), lambda bi, i: (bi, i, 0, 0))
    tok16 = jax.ShapeDtypeStruct((b, s, RWKV_DIM), BF16)
    tok32 = jax.ShapeDtypeStruct((b, s, RWKV_DIM), F32)
    pt32 = jax.ShapeDtypeStruct((b, s // CHUNK, 1, RWKV_DIM), F32)
    per_dir_specs = [tok, tok, tok, tok, tok, tok, tok, ptot]
    per_dir_shapes = [tok16] * 7 + [pt32]
    consts = [mu, w0, w2bd, a0, a2bd, g2, k_k, k_a, r_k, ones_bd]
    return pl.pallas_call(
        functools.partial(_rwkv_prep_body, ts=ts),
        grid=(b, nt),
        in_specs=[main, halo_prev, halo_next] + [_resident(c.shape) for c in consts],
        out_specs=per_dir_specs * 2 + [tok, tok, tok],
        out_shape=per_dir_shapes * 2 + [tok16, tok16, tok16],
        compiler_params=_params("parallel", "parallel"),
        name="rwkv_prep",
    )(p, p, p, *consts)


def _rwkv_scan_body(*refs, nc, nb):
    fwd, bwd = refs[0:8], refs[8:16]
    vf_ref, vb_ref = refs[16:18]
    yf_ref, yb_ref = refs[18:20]
    st_ref = refs[20]
    step = pl.program_id(1)

    @pl.when(step == 0)
    def _():
        st_ref[...] = jnp.zeros_like(st_ref)

    rr = lax.broadcasted_iota(jnp.int32, (QUAD, QUAD), 0) // HEAD_DIM
    cc = lax.broadcasted_iota(jnp.int32, (QUAD, QUAD), 1) // HEAD_DIM
    diag = rr == cc
    dirs = ((fwd, vf_ref, yf_ref), (bwd, vb_ref, yb_ref))
    chains = [(bi, d, qd) for bi in range(nb) for d in range(2) for qd in range(RWKV_DIM // QUAD)]
    st = {ch: st_ref[ch] for ch in chains}
    for j in range(nc):
        z = {}
        for ch in chains:
            bi, d, qd = ch
            c = j if d == 0 else nc - 1 - j
            rs, ls = slice(c * CHUNK, (c + 1) * CHUNK), slice(qd * QUAD, (qd + 1) * QUAD)
            at_ref, rt_ref = dirs[d][0][0:2]
            x2 = jnp.concatenate([at_ref[bi, rs, ls], rt_ref[bi, rs, ls]], axis=0)
            z[ch] = _dot_nt(x2, st[ch].astype(BF16))
        for ch in chains:
            bi, d, qd = ch
            c = j if d == 0 else nc - 1 - j
            rs, ls = slice(c * CHUNK, (c + 1) * CHUNK), slice(qd * QUAD, (qd + 1) * QUAD)
            (_, _, mrb_ref, uv_ref, yv_ref, bp_ref, kp_ref, pt_ref), v_ref, y_ref = dirs[d]
            u = z[ch][:CHUNK] + uv_ref[bi, rs, ls]
            y = z[ch][CHUNK:] + _dot(mrb_ref[bi, rs, ls], _bd16(u)) + yv_ref[bi, rs, ls]
            y_ref[bi, rs, ls] = y.astype(BF16)
            uv2 = jnp.concatenate([u.astype(BF16), v_ref[bi, rs, ls]], axis=0)
            bk2 = jnp.concatenate([bp_ref[bi, rs, ls], kp_ref[bi, rs, ls]], axis=0)
            st[ch] = st[ch] * pt_ref[bi, c][:, ls] + jnp.where(diag, _dot_tn(uv2, bk2), 0.0)
    for ch in chains:
        st_ref[ch] = st[ch]


def _rwkv_scan(fwd_arrs, bwd_arrs, v16):
    b, s, _ = v16.shape
    nc = min(SCAN_CHUNKS, s // CHUNK)
    nb = SCAN_BATCH if b % SCAN_BATCH == 0 else 1
    rows = nc * CHUNK
    steps = s // rows
    f_tok = pl.BlockSpec((nb, rows, RWKV_DIM), lambda bi, i: (bi, i, 0))
    b_tok = pl.BlockSpec((nb, rows, RWKV_DIM), lambda bi, i: (bi, steps - 1 - i, 0))
    f_pt = pl.BlockSpec((nb, nc, 1, RWKV_DIM), lambda bi, i: (bi, i, 0, 0))
    b_pt = pl.BlockSpec((nb, nc, 1, RWKV_DIM), lambda bi, i: (bi, steps - 1 - i, 0, 0))
    y_shape = jax.ShapeDtypeStruct((b, s, RWKV_DIM), BF16)
    return pl.pallas_call(
        functools.partial(_rwkv_scan_body, nc=nc, nb=nb),
        grid=(b // nb, steps),
        in_specs=[f_tok] * 7 + [f_pt] + [b_tok] * 7 + [b_pt] + [f_tok, b_tok],
        out_specs=[f_tok, b_tok],
        out_shape=[y_shape, y_shape],
        scratch_shapes=[pltpu.VMEM((nb, 2, RWKV_DIM // QUAD, QUAD, QUAD), F32)],
        compiler_params=_params("parallel", "arbitrary"),
        name="rwkv_scan",
    )(*fwd_arrs, *bwd_arrs, v16, v16)


N_PAIRS = 4


def _pair_queries(q_ref, pair, rows=slice(None)):
    qp = q_ref[0, rows, pair * LANES:(pair + 1) * LANES]
    lane = lax.broadcasted_iota(jnp.int32, qp.shape, 1)
    zero = jnp.zeros_like(qp)
    return jnp.concatenate([jnp.where(lane < HEAD_DIM, qp, zero), jnp.where(lane >= HEAD_DIM, qp, zero)], axis=0)


def _store_pair(o_ref, pair, o, n, rows=slice(None)):
    lane = lax.broadcasted_iota(jnp.int32, (n, LANES), 1)
    o_ref[0, rows, pair * LANES:(pair + 1) * LANES] = jnp.where(lane < HEAD_DIM, o[:n], o[n:]).astype(BF16)


def _values_and_ones(vd):
    return jnp.concatenate([vd, jnp.ones_like(vd)], axis=1)


def _skewed_pairs(scores, softmax, finish, pieces=tuple(range(N_PAIRS))):
    sc, pr = {}, {}
    n = len(pieces)
    for step in range(n + 2):
        if step < n:
            sc[step] = scores(pieces[step])
        if 1 <= step <= n:
            pr[step - 1] = softmax(pieces[step - 1], sc.pop(step - 1))
        if step >= 2:
            finish(pieces[step - 2], pr.pop(step - 2))


def _ga_body(q_ref, kv_ref, o_ref, *, tq):
    blk = min(GA_Q_BLOCK, tq)
    rows = lambda j: slice(j * blk, (j + 1) * blk)

    def scores(piece):
        p, j = piece
        return _dot_nt(_pair_queries(q_ref, p, rows(j)), kv_ref[0, :, (p // 2) * LANES:(p // 2 + 1) * LANES])

    def softmax(piece, sc):
        return jnp.exp2(sc - jnp.max(sc, axis=-1, keepdims=True)).astype(BF16)

    def finish(piece, e):
        p, j = piece
        ov = _dot(e, _values_and_ones(kv_ref[0, :, (2 + p // 2) * LANES:(3 + p // 2) * LANES]))
        _store_pair(o_ref, p, ov[:, :LANES] / ov[:, LANES:], blk, rows(j))

    _skewed_pairs(scores, softmax, finish, [(p, j) for p in range(N_PAIRS) for j in range(tq // blk)])


def _global_attention(q, kv):
    b, s, w = q.shape
    tq = min(GA_Q_TILE, s)
    qs = pl.BlockSpec((1, tq, w), lambda bi, i: (bi, i, 0))
    kvs = pl.BlockSpec((1, s, w), lambda bi, i: (bi, 0, 0))
    return pl.pallas_call(
        functools.partial(_ga_body, tq=tq),
        grid=(b, s // tq),
        in_specs=[qs, kvs],
        out_specs=qs,
        out_shape=jax.ShapeDtypeStruct((b, s, w), BF16),
        compiler_params=_params("parallel", "arbitrary"),
        name="global_attn",
    )(q, kv)


def _wa_body(sink_ref, q_ref, kv_ref, *refs, blk, span, seq, nsub):
    band_refs, o_ref = refs[:nsub], refs[nsub]
    first = pl.program_id(1) * nsub
    starts = [pl.multiple_of(jnp.clip((first + j) * blk - WINDOW, 0, seq - span), WINDOW) for j in range(nsub)]
    valid = []
    for j in range(nsub):
        band = band_refs[j][0] > 0.5
        valid.append(jnp.concatenate([band, band], axis=0))

    def scores(piece):
        p, j = piece
        kd = kv_ref[0, pl.ds(starts[j], span), (p // 2) * LANES:(p // 2 + 1) * LANES]
        sc = _dot_nt(_pair_queries(q_ref, p, slice(j * blk, (j + 1) * blk)), kd)
        return jnp.where(valid[j], sc, NEG_INF)

    def softmax(piece, sc):
        p, _ = piece
        e, sink_e = [], []
        for h in range(2):
            sc_h = sc[h * blk:(h + 1) * blk]
            sink = sink_ref[2 * p + h] * LOG2E
            m = jnp.maximum(jnp.max(sc_h, axis=-1, keepdims=True), sink)
            e.append(jnp.exp2(sc_h - m).astype(BF16))
            sink_e.append(jnp.exp2(sink - m))
        return jnp.concatenate(e, axis=0), jnp.concatenate(sink_e, axis=0)

    def finish(piece, es):
        p, j = piece
        e, sink_e = es
        vd = kv_ref[0, pl.ds(starts[j], span), (2 + p // 2) * LANES:(3 + p // 2) * LANES]
        ov = _dot(e, _values_and_ones(vd))
        _store_pair(o_ref, p, ov[:, :LANES] / (ov[:, LANES:] + sink_e), blk, slice(j * blk, (j + 1) * blk))

    _skewed_pairs(scores, softmax, finish, [(p, j) for p in range(N_PAIRS) for j in range(nsub)])


def _window_attention(q, kv, sink):
    b, s, w = q.shape
    blk = min(WA_Q_BLOCK, s)
    span = min(blk + 2 * WINDOW, s)
    nblk = s // blk
    nsub = 2 if nblk % 2 == 0 else 1
    offsets = sorted({int(np.clip(i * blk - WINDOW, 0, s - span)) - i * blk for i in range(nblk)}, reverse=True)
    rel = jnp.arange(span)[None, :] - jnp.arange(blk)[:, None]
    band = jnp.stack([(jnp.abs(rel + off) <= WINDOW).astype(F32) for off in offsets])
    which = lambda i: jnp.where(i == 0, 0, jnp.where(i == nblk - 1, len(offsets) - 1, min(1, len(offsets) - 1)))
    qs = pl.BlockSpec((1, nsub * blk, w), lambda bi, i, *_: (bi, i, 0))
    kvs = pl.BlockSpec((1, s, w), lambda bi, i, *_: (bi, 0, 0))
    bands = [pl.BlockSpec((1, blk, span), functools.partial(lambda j, bi, i, *_: (which(i * nsub + j), 0, 0), j))
             for j in range(nsub)]
    return pl.pallas_call(
        functools.partial(_wa_body, blk=blk, span=span, seq=s, nsub=nsub),
        grid_spec=pltpu.PrefetchScalarGridSpec(
            num_scalar_prefetch=1, grid=(b, nblk // nsub), in_specs=[qs, kvs] + bands, out_specs=qs),
        out_shape=jax.ShapeDtypeStruct((b, s, w), BF16),
        compiler_params=_params("parallel", "arbitrary"),
        name="window_attn",
    )(sink, q, kv, *([band] * nsub))


def _merge_body(x_ref, g_ref, wg_ref, wf_ref, wb_ref, bonus_ref, rgate_ref, lg_ref, lb_ref, ones_ref,
                yb_ref, yc_ref, woa_ref, wob_ref, woc_ref, wout_ref, o_ref):
    ones = ones_ref[...]

    def prepare(rs):
        h = _rmsnorm(x_ref[rs, :], g_ref[...]).astype(BF16)
        wkv = wf_ref[rs, :].astype(F32) + wb_ref[rs, :].astype(F32)
        dev = wkv - _head_sum(wkv, ones, 2) * (1.0 / HEAD_DIM)
        var = _head_sum(dev * dev, ones, 1) * (1.0 / HEAD_DIM)
        y_a = dev * lax.rsqrt(var + LNX_EPS) * lg_ref[...] + lb_ref[...] + bonus_ref[rs, :].astype(F32)
        return h, (y_a * rgate_ref[rs, :].astype(F32)).astype(BF16)

    def gated_sum(rs, h, y_a):
        gates = jax.nn.sigmoid(_dot(h, wg_ref[...]))
        merged = None
        for j, (y, wo_ref) in enumerate(((y_a, woa_ref), (yb_ref[rs, :], wob_ref), (yc_ref[rs, :], woc_ref))):
            term = gates[:, j * D_MODEL:(j + 1) * D_MODEL] * _dot(y, wo_ref[...])
            merged = term if merged is None else merged + term
        return merged.astype(BF16)

    parts = _row_parts(x_ref.shape[0])
    prepared = [prepare(rs) for rs in parts]
    merged = [gated_sum(rs, *pr) for rs, pr in zip(parts, prepared)]
    for rs, m in zip(parts, merged):
        o_ref[rs, :] = x_ref[rs, :] + _dot(m, wout_ref[...])


def _merge(x, g, wg, wkv_f, wkv_b, bonus, rgate, lnx_g, lnx_b, ones_bd, yb, yc, woa, wob, woc, wout):
    t, d = x.shape
    tm = min(TOKEN_TILE, t)
    row = lambda w_: pl.BlockSpec((tm, w_), lambda i: (i, 0))
    half = row(RWKV_DIM)
    return pl.pallas_call(
        _merge_body,
        grid=(t // tm,),
        in_specs=[row(d), _resident((1, d)), _resident(wg.shape), half, half, half, half,
                  _resident(lnx_g.shape), _resident(lnx_b.shape), _resident(ones_bd.shape), half, half,
                  _resident(woa.shape), _resident(wob.shape), _resident(woc.shape), _resident(wout.shape)],
        out_specs=row(d),
        out_shape=jax.ShapeDtypeStruct((t, d), F32),
        compiler_params=_params("parallel"),
        name="merge",
    )(x, g, wg, wkv_f, wkv_b, bonus, rgate, lnx_g, lnx_b, ones_bd, yb, yc, woa, wob, woc, wout)


def _rope_tables(seq):
    t = jnp.arange(seq)

    def angles(pos, dim):
        inv_freq = ROPE_THETA ** (-jnp.arange(0, dim, 2, dtype=F32) / dim)
        return pos.astype(F32)[:, None] * inv_freq[None, :]

    ang_row, ang_col = angles(t // GRID_W, HEAD_DIM // 2), angles(t % GRID_W, HEAD_DIM // 2)
    ang_1d = angles(t, HEAD_DIM)
    ga_cos = jnp.concatenate([jnp.cos(ang_row)] * 2 + [jnp.cos(ang_col)] * 2, axis=1)
    ga_sin = jnp.concatenate([-jnp.sin(ang_row), jnp.sin(ang_row), -jnp.sin(ang_col), jnp.sin(ang_col)], axis=1)
    wa_cos = jnp.concatenate([jnp.cos(ang_1d)] * 2, axis=1)
    wa_sin = jnp.concatenate([-jnp.sin(ang_1d), jnp.sin(ang_1d)], axis=1)
    return tuple(jnp.concatenate([a, a], axis=1) for a in (ga_cos, ga_sin, wa_cos, wa_sin))


def _two_way(w):
    z = jnp.zeros_like(w[0])
    return jnp.concatenate([jnp.concatenate([w[0], z], axis=1), jnp.concatenate([z, w[1]], axis=1)], axis=0)


def kernel(x, ffn1_norm, ffn1_w_up, ffn1_w_down, mix_norm, w_in, rwkv_mu, rwkv_w0, rwkv_w2, rwkv_a0, rwkv_a2, rwkv_g2, rwkv_k_k, rwkv_k_a, rwkv_r_k, rwkv_lnx_g, rwkv_lnx_b, ga_q_norm, ga_k_norm, wa_sink, w_o_rwkv, w_o_ga, w_o_wa, w_out, ffn2_norm, ffn2_w_up, ffn2_w_down, final_norm):
    bsz, seq, dm = x.shape
    depth = w_in.shape[0]
    tokens = bsz * seq
    tabs = _rope_tables(seq)
    idx = jnp.arange(RWKV_DIM)
    ones_bd = (idx[:, None] // HEAD_DIM == idx[None, :] // HEAD_DIM).astype(BF16)
    row1 = lambda a: a.reshape(1, -1)
    final_g = row1(final_norm)

    xt = x.reshape(tokens, dm)
    for l in range(depth):
        xt = _ffn(xt, row1(ffn1_norm[l]), ffn1_w_up[l].astype(BF16), ffn1_w_down[l].astype(BF16), final_g, False)

        w_mix = w_in[l, :, :MIX_COLS].astype(BF16)
        w_gate = w_in[l, :, MIX_COLS:].astype(BF16)
        p_rwkv, ga_q, ga_kv, wa_q, wa_kv = _inproj(
            xt, row1(mix_norm[l]), w_mix, ones_bd, row1(jnp.tile(ga_q_norm[l], 8)), row1(jnp.tile(ga_k_norm[l], 8)),
            tabs, seq)

        outs = _rwkv_prep(
            p_rwkv.reshape(bsz, seq, RWKV_COLS), row1(rwkv_mu[l]), row1(rwkv_w0[l]), _two_way(rwkv_w2[l]).astype(BF16),
            row1(rwkv_a0[l]), _two_way(rwkv_a2[l]).astype(BF16), rwkv_g2[l].astype(BF16), row1(rwkv_k_k[l]),
            row1(rwkv_k_a[l]), row1(rwkv_r_k[l]), ones_bd)
        v16, bonus, gate = outs[16:19]
        y_f, y_b = _rwkv_scan(outs[0:8], outs[8:16], v16)
        flat = lambda a: a.reshape(tokens, RWKV_DIM)

        per_seq = lambda a: a.reshape(bsz, seq, ATT_Q)
        y_g = _global_attention(per_seq(ga_q), per_seq(ga_kv))
        y_w = _window_attention(per_seq(wa_q), per_seq(wa_kv), wa_sink[l])

        xt = _merge(xt, row1(mix_norm[l]), w_gate, flat(y_f), flat(y_b), flat(bonus), flat(gate),
                    row1(rwkv_lnx_g[l]), row1(rwkv_lnx_b[l]), ones_bd, flat(y_g), flat(y_w),
                    w_o_rwkv[l].astype(BF16), w_o_ga[l].astype(BF16), w_o_wa[l].astype(BF16), w_out[l].astype(BF16))

        xt = _ffn(xt, row1(ffn2_norm[l]), ffn2_w_up[l].astype(BF16), ffn2_w_down[l].astype(BF16), final_g,
                  l == depth - 1)
    return xt.reshape(bsz, seq, dm)
```

```python
import functools

import jax
import jax.numpy as jnp
import numpy as np
from jax import lax
from jax.experimental import pallas as pl
from jax.experimental.pallas import tpu as pltpu

F32 = jnp.float32
BF16 = jnp.bfloat16

D_MODEL = 1024
HEAD_DIM = 64
RWKV_DIM = 512
RWKV_COLS = 1920
ATT_Q = 8 * HEAD_DIM
ATT_KV = 2 * HEAD_DIM
GA_COLS = ATT_Q + 2 * ATT_KV
WA_COLS = ATT_Q + 2 * ATT_KV
MIX_COLS = RWKV_COLS + GA_COLS + WA_COLS
D_FF = 2816
GRID_W = 64
WINDOW = 128
ROPE_THETA = 10000.0
NORM_EPS = 1e-6
LNX_EPS = HEAD_DIM * 1e-5
NEG_INF = -1e30
LOG2E = float(np.log2(np.e))
ATTN_SCALE_LOG2 = HEAD_DIM ** -0.5 * LOG2E

CHUNK = 64
QUAD = 4 * HEAD_DIM
LANES = 128
VMEM_LIMIT = 56 * 1024 * 1024

TOKEN_TILE = 1024
ROW_PART = 256
INPROJ_PART = 128
FF_CHUNK = 2816
PREP_TILE = 512
PREP_SUB = 128
SCAN_CHUNKS = 8
SCAN_BATCH = 2
GA_Q_BLOCK = 256
GA_Q_TILE = 512
WA_Q_BLOCK = 128
WA_SUB_BLOCKS = 4


def _params(*sem):
    return pltpu.CompilerParams(dimension_semantics=sem, vmem_limit_bytes=VMEM_LIMIT)


def _resident(shape):
    nd = len(shape)
    return pl.BlockSpec(shape, lambda *_: (0,) * nd, pipeline_mode=pl.Buffered(1))


def _rmsnorm(x, g):
    return x * lax.rsqrt(jnp.mean(x * x, axis=-1, keepdims=True) + NORM_EPS) * g


def _row_parts(rows, part=None):
    part = min(part or ROW_PART, rows // 2) if rows % 32 == 0 else rows
    return [slice(r, r + part) for r in range(0, rows, part)]


def _dot(a, b):
    return jnp.dot(a, b, preferred_element_type=F32)


def _dot_nt(a, b):
    return lax.dot_general(a, b, (((1,), (1,)), ((), ())), preferred_element_type=F32)


def _dot_tn(a, b):
    return lax.dot_general(a, b, (((0,), (0,)), ((), ())), preferred_element_type=F32)


def _split_bf16(x, terms):
    out = []
    for _ in range(terms - 1):
        hi = x.astype(BF16)
        out.append(hi)
        x = x - hi.astype(F32)
    out.append(x.astype(BF16))
    return out


def _head_sum(x, ones_bd, terms):
    w = x.shape[-1]
    blk = min(w, QUAD)
    ones = ones_bd[:blk, :blk]
    parts = []
    for c0 in range(0, w, blk):
        acc = None
        for t in _split_bf16(x[:, c0:c0 + blk], terms):
            y = _dot(t, ones)
            acc = y if acc is None else acc + y
        parts.append(acc)
    return parts[0] if len(parts) == 1 else jnp.concatenate(parts, axis=1)


def _ffn_body(x_ref, g_ref, wup_ref, wdn_ref, gf_ref, o_ref, act_ref, *, final_norm):
    halves = _row_parts(x_ref.shape[0])
    h = [_rmsnorm(x_ref[rs, :], g_ref[...]).astype(BF16) for rs in halves]
    for c in range(D_FF // FF_CHUNK):
        lo, hi = c * FF_CHUNK, (c + 1) * FF_CHUNK
        for rs, hh in zip(halves, h):
            gate = _dot(hh, wup_ref[:, lo:hi])
            up = _dot(hh, wup_ref[:, D_FF + lo:D_FF + hi])
            act_ref[rs, lo:hi] = (gate * jax.nn.sigmoid(gate) * up).astype(BF16)
    for rs in halves:
        y = x_ref[rs, :] + 0.5 * _dot(act_ref[rs, :], wdn_ref[...])
        if final_norm:
            y = _rmsnorm(y, gf_ref[...])
        o_ref[rs, :] = y


def _ffn(x, g, w_up, w_down, g_final, final_norm):
    t, d = x.shape
    tm = min(TOKEN_TILE, t)
    row = pl.BlockSpec((tm, d), lambda i: (i, 0))
    return pl.pallas_call(
        functools.partial(_ffn_body, final_norm=final_norm),
        grid=(t // tm,),
        in_specs=[row, _resident((1, d)), _resident(w_up.shape), _resident(w_down.shape), _resident((1, d))],
        out_specs=row,
        out_shape=jax.ShapeDtypeStruct((t, d), F32),
        scratch_shapes=[pltpu.VMEM((tm, D_FF), BF16)],
        compiler_params=_params("parallel"),
        name="ffn_final" if final_norm else "ffn",
    )(x, g, w_up, w_down, g_final)


def _rope(t, cos, sin_signed, half):
    w = t.shape[-1]
    lane = lax.broadcasted_iota(jnp.int32, t.shape, 1)
    first = (lane % (2 * half)) < half
    partner = jnp.where(first, pltpu.roll(t, w - half, 1), pltpu.roll(t, half, 1))
    return t * cos + partner * sin_signed


def _dup_heads(t):
    lane = lax.broadcasted_iota(jnp.int32, t.shape, 1)
    swapped = pltpu.roll(t, HEAD_DIM, 1)
    low = lane < HEAD_DIM
    return jnp.where(low, t, swapped), jnp.where(low, swapped, t)


def _inproj_body(x_ref, g_ref, w_ref, ones_ref, gq_ref, gk_ref, cga_ref, sga_ref, cwa_ref, swa_ref,
                 prw_ref, gaq_ref, gakv_ref, waq_ref, wakv_ref):
    ones = ones_ref[...]

    def head_norm(t, gain):
        ms = _head_sum(t * t, ones, 1) * (1.0 / HEAD_DIM)
        return t * lax.rsqrt(ms + NORM_EPS) * gain

    tile4 = lambda tab: jnp.concatenate([tab] * 4, axis=1)

    def project(rs):
        h = _rmsnorm(x_ref[rs, :], g_ref[...]).astype(BF16)
        prw_ref[rs, :] = _dot(h, w_ref[:, :RWKV_COLS])
        return _dot(h, w_ref[:, RWKV_COLS:RWKV_COLS + GA_COLS]), _dot(h, w_ref[:, RWKV_COLS + GA_COLS:MIX_COLS])

    def finish(rs, ga, wa):
        cga, sga = cga_ref[rs, :], sga_ref[rs, :]
        q = _rope(head_norm(ga[:, :ATT_Q], gq_ref[...]), tile4(cga), tile4(sga), 16)
        k = _rope(head_norm(ga[:, ATT_Q:ATT_Q + ATT_KV], gk_ref[:, :ATT_KV]), cga, sga, 16)
        gaq_ref[rs, :] = (q * ATTN_SCALE_LOG2).astype(BF16)
        k0, k1 = _dup_heads(k)
        v0, v1 = _dup_heads(ga[:, ATT_Q + ATT_KV:])
        gakv_ref[rs, :] = jnp.concatenate([k0, k1, v0, v1], axis=1).astype(BF16)
        cwa, swa = cwa_ref[rs, :], swa_ref[rs, :]
        q = _rope(wa[:, :ATT_Q], tile4(cwa), tile4(swa), 32)
        k = _rope(wa[:, ATT_Q:ATT_Q + ATT_KV], cwa, swa, 32)
        waq_ref[rs, :] = (q * ATTN_SCALE_LOG2).astype(BF16)
        k0, k1 = _dup_heads(k)
        v0, v1 = _dup_heads(wa[:, ATT_Q + ATT_KV:])
        wakv_ref[rs, :] = jnp.concatenate([k0, k1, v0, v1], axis=1).astype(BF16)

    parts = _row_parts(x_ref.shape[0], INPROJ_PART)
    projected = {}
    for j in range(len(parts) + 1):
        if j < len(parts):
            projected[j] = project(parts[j])
        if j >= 1:
            finish(parts[j - 1], *projected.pop(j - 1))


def _inproj(x, g, w, ones_bd, gq, gk, tabs, seq):
    t, d = x.shape
    tm = min(TOKEN_TILE, seq)
    per_seq = seq // tm
    row = lambda w_: pl.BlockSpec((tm, w_), lambda i: (i, 0))
    tab = pl.BlockSpec((tm, LANES), lambda i: (i % per_seq, 0))
    return pl.pallas_call(
        _inproj_body,
        grid=(t // tm,),
        in_specs=[row(d), _resident((1, d)), _resident(w.shape), _resident(ones_bd.shape),
                  _resident(gq.shape), _resident(gk.shape), tab, tab, tab, tab],
        out_specs=[row(RWKV_COLS)] + [row(ATT_Q)] * 4,
        out_shape=[jax.ShapeDtypeStruct((t, RWKV_COLS), F32)] + [jax.ShapeDtypeStruct((t, ATT_Q), BF16)] * 4,
        compiler_params=_params("parallel"),
        name="inproj",
    )(x, g, w, ones_bd, gq, gk, *tabs)


def _block_diag4(x):
    head = lax.broadcasted_iota(jnp.int32, x.shape, 1) // HEAD_DIM
    return jnp.concatenate([jnp.where(head == h, x, jnp.zeros_like(x)) for h in range(4)], axis=0)


def _bd16(x):
    return _block_diag4(x).astype(BF16)


def _rwkv_prep_body(p_ref, hp_ref, hn_ref, mu_ref, w0_ref, w2_ref, a0_ref, a2_ref, g2_ref, kk_ref, ka_ref, rk_ref,
                    ones_ref, *out_refs, ts):
    i = pl.program_id(1)
    n = pl.num_programs(1)
    edge_prev = jnp.where(i > 0, hp_ref[0, 7:8, :], 0.0)
    edge_next = jnp.where(i < n - 1, hn_ref[0, 0:1, :], 0.0)
    consts = (mu_ref, w0_ref, w2_ref, a0_ref, a2_ref, g2_ref, kk_ref, ka_ref, rk_ref, ones_ref)
    chain_stage = None
    for r0 in range(0, ts, PREP_SUB):
        prev_row = edge_prev if r0 == 0 else p_ref[0, r0 - 1:r0, :]
        next_row = edge_next if r0 + PREP_SUB == ts else p_ref[0, r0 + PREP_SUB:r0 + PREP_SUB + 1, :]
        state = {}
        vector_stage = _prep_vector_stage(p_ref, prev_row, next_row, consts, out_refs, r0, PREP_SUB, state)
        _alternate(chain_stage, vector_stage)
        chain_stage = _prep_chain_stage(state, out_refs, r0, PREP_SUB)
    _alternate(chain_stage, None)


def _alternate(first, second):
    live = [g for g in (first, second) if g is not None]
    while live:
        for g in list(live):
            try:
                next(g)
            except StopIteration:
                live.remove(g)


def _prep_vector_stage(p_ref, prev_row, next_row, consts, out_refs, r0, ts, state):
    mu_ref, w0_ref, w2_ref, a0_ref, a2_ref, g2_ref, kk_ref, ka_ref, rk_ref, ones_ref = consts
    dir_refs = (out_refs[0:8], out_refs[8:16])
    v_ref, bonus_ref, gate_ref = out_refs[16:19]
    rows = slice(r0, r0 + ts)

    def shifted_mix(lo, hi):
        p = p_ref[0, rows, lo:hi]
        row = lax.broadcasted_iota(jnp.int32, p.shape, 0)
        prev = jnp.where(row == 0, prev_row[:, lo:hi], pltpu.roll(p, 1, 0))
        nxt = jnp.where(row == ts - 1, next_row[:, lo:hi], pltpu.roll(p, ts - 1, 0))
        return p + mu_ref[:, lo:hi] * (0.5 * (prev + nxt) - p)

    lora = shifted_mix(3 * RWKV_DIM, RWKV_COLS)
    wl, al, gl = lora[:, 0:LANES], lora[:, LANES:2 * LANES], lora[:, 2 * LANES:3 * LANES]
    yield
    r = shifted_mix(0, RWKV_DIM)
    yield
    k = shifted_mix(RWKV_DIM, 2 * RWKV_DIM)
    yield
    v = shifted_mix(2 * RWKV_DIM, 3 * RWKV_DIM)
    yield
    ones = ones_ref[...]

    u = w0_ref[...] + _dot(jnp.tanh(wl).astype(BF16), w2_ref[...])
    logw = (-float(np.exp(-0.5))) * jax.nn.sigmoid(u)
    asig = jax.nn.sigmoid(a0_ref[...] + _dot(al.astype(BF16), a2_ref[...]))
    yield
    kkr = k * kk_ref[...]
    kk = kkr * lax.rsqrt(jnp.maximum(_head_sum(kkr * kkr, ones, 1), 1e-24))
    k_a = ka_ref[...]
    kd = [k * (1.0 + (asig[:, d * RWKV_DIM:(d + 1) * RWKV_DIM] - 1.0) * k_a) for d in range(2)]
    yield
    bonus_ref[0, rows] = (_head_sum(r * rk_ref[...] * (kd[0] + kd[1]), ones, 1) * v).astype(BF16)
    gate_ref[0, rows] = _dot(jax.nn.sigmoid(gl).astype(BF16), g2_ref[...]).astype(BF16)
    v_ref[0, rows] = v.astype(BF16)
    yield

    rr = lax.broadcasted_iota(jnp.int32, (ts, ts), 0)
    cc = lax.broadcasted_iota(jnp.int32, (ts, ts), 1)
    same = (rr // CHUNK) == (cc // CHUNK)

    scaled = []
    for d in range(2):
        bp_ref, kp_ref, pt_ref = dir_refs[d][5:8]
        rt_ref = dir_refs[d][1]
        before = (cc <= rr) if d == 0 else (cc >= rr)
        lhs = jnp.where(same & before, 1.0, 0.0).astype(BF16)
        lw = logw[:, d * RWKV_DIM:(d + 1) * RWKV_DIM]
        cl = None
        for term in _split_bf16(lw, 2):
            y = _dot(lhs, term)
            cl = y if cl is None else cl + y
        yield
        ends = [c * CHUNK + (CHUNK - 1 if d == 0 else 0) for c in range(ts // CHUNK)]
        p_tot = [jnp.exp(cl[e:e + 1, :]) for e in ends]
        for c in range(ts // CHUNK):
            pt_ref[0, r0 // CHUNK + c] = p_tot[c]
        e_neg = jnp.exp(-cl)
        b_raw = kk * asig[:, d * RWKV_DIM:(d + 1) * RWKV_DIM]
        r_t = r * jnp.exp(cl)
        rt_ref[0, rows] = r_t.astype(BF16)
        yield
        e_out = e_neg * jnp.concatenate([jnp.broadcast_to(pc, (CHUNK, RWKV_DIM)) for pc in p_tot], axis=0)
        bp_ref[0, rows] = (b_raw * e_out).astype(BF16)
        kp_ref[0, rows] = (kd[d] * e_out).astype(BF16)
        yield
        scaled.append((-kk * jnp.exp(cl - lw), b_raw * e_neg, kd[d] * e_neg, r_t))
        yield
    state.update(scaled=scaled, v=v)


def _prep_chain_stage(state, out_refs, r0, ts):
    scaled, v = state["scaled"], state["v"]
    dir_refs = (out_refs[0:8], out_refs[8:16])
    tt = lax.broadcasted_iota(jnp.int32, (CHUNK, QUAD), 0)
    ss = lax.broadcasted_iota(jnp.int32, (CHUNK, QUAD), 1) % CHUNK
    eye = jnp.where(tt == ss, 1.0, 0.0)
    chains = [(d, c, qd) for d in range(2) for c in range(ts // CHUNK) for qd in range(RWKV_DIM // QUAD)]
    cut = lambda arr, c, qd: arr[c * CHUNK:(c + 1) * CHUNK, qd * QUAD:(qd + 1) * QUAD]
    strict = [ss < tt, ss > tt]
    incl = [ss <= tt, ss >= tt]
    at, x2 = {}, {}
    for ch in chains:
        d, c, qd = ch
        at[ch] = cut(scaled[d][0], c, qd)
        x2[ch] = jnp.concatenate([at[ch], cut(scaled[d][3], c, qd)], axis=0).astype(BF16)
    w_b = {ch: _dot_nt(x2[ch], _bd16(cut(scaled[ch[0]][1], ch[1], ch[2]))) for ch in chains}
    yield
    w_k = {ch: _dot_nt(x2[ch], _bd16(cut(scaled[ch[0]][2], ch[1], ch[2]))) for ch in chains}
    yield
    a_ab, gy = {}, {}
    for ch in chains:
        d, c, qd = ch
        a_ab[ch] = jnp.where(strict[d], w_b[ch][:CHUNK], 0.0)
        dir_refs[d][2][0, r0 + c * CHUNK:r0 + (c + 1) * CHUNK, qd * QUAD:(qd + 1) * QUAD] = (
            jnp.where(incl[d], w_b[ch][CHUNK:], 0.0).astype(BF16))
        lk = jnp.concatenate([jnp.where(strict[d], w_k[ch][:CHUNK], 0.0),
                              jnp.where(incl[d], w_k[ch][CHUNK:], 0.0)], axis=0)
        gy[ch] = _dot(lk.astype(BF16), _bd16(cut(v, c, qd)))
    yield
    tm = {ch: eye + a_ab[ch] for ch in chains}
    pw = {ch: _dot(a_ab[ch].astype(BF16), _bd16(a_ab[ch])) for ch in chains}
    yield
    for _ in range(4):
        both = {ch: _dot(jnp.concatenate([pw[ch], tm[ch]], axis=0).astype(BF16), _bd16(pw[ch])) for ch in chains}
        tm = {ch: tm[ch] + both[ch][CHUNK:] for ch in chains}
        pw = {ch: both[ch][:CHUNK] for ch in chains}
        yield
    tm = {ch: (tm[ch] + _dot(tm[ch].astype(BF16), _bd16(pw[ch]))).astype(BF16) for ch in chains}
    yield
    for ch in chains:
        d, c, qd = ch
        rs, ls = slice(r0 + c * CHUNK, r0 + (c + 1) * CHUNK), slice(qd * QUAD, (qd + 1) * QUAD)
        at_ref, _, _, uv_ref, yv_ref = dir_refs[d][0:5]
        uv_ref[0, rs, ls] = _dot(tm[ch], _bd16(gy[ch][:CHUNK])).astype(BF16)
        at_ref[0, rs, ls] = _dot(tm[ch], _bd16(at[ch])).astype(BF16)
        yv_ref[0, rs, ls] = gy[ch][CHUNK:].astype(BF16)


def _rwkv_prep(p, mu, w0, w2bd, a0, a2bd, g2, k_k, k_a, r_k, ones_bd):
    b, s, _ = p.shape
    ts = min(PREP_TILE, s)
    nt = s // ts
    h8 = ts // 8
    main = pl.BlockSpec((1, ts, RWKV_COLS), lambda bi, i: (bi, i, 0))
    halo_prev = pl.BlockSpec((1, 8, RWKV_COLS), lambda bi, i: (bi, jnp.maximum(i * h8 - 1, 0), 0))
    halo_next = pl.BlockSpec((1, 8, RWKV_COLS), lambda bi, i: (bi, jnp.minimum((i + 1) * h8, s // 8 - 1), 0))
    tok = pl.BlockSpec((1, ts, RWKV_DIM), lambda bi, i: (bi, i, 0))
    ptot = pl.BlockSpec((1, ts // CHUNK, 1, RWKV_DIM), lambda bi, i: (bi, i, 0, 0))
    tok16 = jax.ShapeDtypeStruct((b, s, RWKV_DIM), BF16)
    tok32 = jax.ShapeDtypeStruct((b, s, RWKV_DIM), F32)
    pt32 = jax.ShapeDtypeStruct((b, s // CHUNK, 1, RWKV_DIM), F32)
    per_dir_specs = [tok, tok, tok, tok, tok, tok, tok, ptot]
    per_dir_shapes = [tok16] * 7 + [pt32]
    consts = [mu, w0, w2bd, a0, a2bd, g2, k_k, k_a, r_k, ones_bd]
    return pl.pallas_call(
        functools.partial(_rwkv_prep_body, ts=ts),
        grid=(b, nt),
        in_specs=[main, halo_prev, halo_next] + [_resident(c.shape) for c in consts],
        out_specs=per_dir_specs * 2 + [tok, tok, tok],
        out_shape=per_dir_shapes * 2 + [tok16, tok16, tok16],
        compiler_params=_params("parallel", "parallel"),
        name="rwkv_prep",
    )(p, p, p, *consts)


def _rwkv_scan_body(*refs, nc, nb):
    fwd, bwd = refs[0:8], refs[8:16]
    vf_ref, vb_ref = refs[16:18]
    yf_ref, yb_ref = refs[18:20]
    st_ref = refs[20]
    step = pl.program_id(1)

    @pl.when(step == 0)
    def _():
        st_ref[...] = jnp.zeros_like(st_ref)

    rr = lax.broadcasted_iota(jnp.int32, (QUAD, QUAD), 0) // HEAD_DIM
    cc = lax.broadcasted_iota(jnp.int32, (QUAD, QUAD), 1) // HEAD_DIM
    diag = rr == cc
    dirs = ((fwd, vf_ref, yf_ref), (bwd, vb_ref, yb_ref))
    chains = [(bi, d, qd) for bi in range(nb) for d in range(2) for qd in range(RWKV_DIM // QUAD)]
    st = {ch: st_ref[ch] for ch in chains}
    for j in range(nc):
        z = {}
        for ch in chains:
            bi, d, qd = ch
            c = j if d == 0 else nc - 1 - j
            rs, ls = slice(c * CHUNK, (c + 1) * CHUNK), slice(qd * QUAD, (qd + 1) * QUAD)
            at_ref, rt_ref = dirs[d][0][0:2]
            x2 = jnp.concatenate([at_ref[bi, rs, ls], rt_ref[bi, rs, ls]], axis=0)
            z[ch] = _dot_nt(x2, st[ch].astype(BF16))
        for ch in chains:
            bi, d, qd = ch
            c = j if d == 0 else nc - 1 - j
            rs, ls = slice(c * CHUNK, (c + 1) * CHUNK), slice(qd * QUAD, (qd + 1) * QUAD)
            (_, _, mrb_ref, uv_ref, yv_ref, bp_ref, kp_ref, pt_ref), v_ref, y_ref = dirs[d]
            u = z[ch][:CHUNK] + uv_ref[bi, rs, ls]
            y = z[ch][CHUNK:] + _dot(mrb_ref[bi, rs, ls], _bd16(u)) + yv_ref[bi, rs, ls]
            y_ref[bi, rs, ls] = y.astype(BF16)
            uv2 = jnp.concatenate([u.astype(BF16), v_ref[bi, rs, ls]], axis=0)
            bk2 = jnp.concatenate([bp_ref[bi, rs, ls], kp_ref[bi, rs, ls]], axis=0)
            st[ch] = st[ch] * pt_ref[bi, c][:, ls] + jnp.where(diag, _dot_tn(uv2, bk2), 0.0)
    for ch in chains:
        st_ref[ch] = st[ch]


def _rwkv_scan(fwd_arrs, bwd_arrs, v16):
    b, s, _ = v16.shape
    nc = min(SCAN_CHUNKS, s // CHUNK)
    nb = SCAN_BATCH if b % SCAN_BATCH == 0 else 1
    rows = nc * CHUNK
    steps = s // rows
    f_tok = pl.BlockSpec((nb, rows, RWKV_DIM), lambda bi, i: (bi, i, 0))
    b_tok = pl.BlockSpec((nb, rows, RWKV_DIM), lambda bi, i: (bi, steps - 1 - i, 0))
    f_pt = pl.BlockSpec((nb, nc, 1, RWKV_DIM), lambda bi, i: (bi, i, 0, 0))
    b_pt = pl.BlockSpec((nb, nc, 1, RWKV_DIM), lambda bi, i: (bi, steps - 1 - i, 0, 0))
    y_shape = jax.ShapeDtypeStruct((b, s, RWKV_DIM), BF16)
    return pl.pallas_call(
        functools.partial(_rwkv_scan_body, nc=nc, nb=nb),
        grid=(b // nb, steps),
        in_specs=[f_tok] * 7 + [f_pt] + [b_tok] * 7 + [b_pt] + [f_tok, b_tok],
        out_specs=[f_tok, b_tok],
        out_shape=[y_shape, y_shape],
        scratch_shapes=[pltpu.VMEM((nb, 2, RWKV_DIM // QUAD, QUAD, QUAD), F32)],
        compiler_params=_params("parallel", "arbitrary"),
        name="rwkv_scan",
    )(*fwd_arrs, *bwd_arrs, v16, v16)


N_PAIRS = 4


def _pair_queries(q_ref, pair, rows=slice(None)):
    qp = q_ref[0, rows, pair * LANES:(pair + 1) * LANES]
    lane = lax.broadcasted_iota(jnp.int32, qp.shape, 1)
    zero = jnp.zeros_like(qp)
    return jnp.concatenate([jnp.where(lane < HEAD_DIM, qp, zero), jnp.where(lane >= HEAD_DIM, qp, zero)], axis=0)


def _store_pair(o_ref, pair, o, n, rows=slice(None)):
    lane = lax.broadcasted_iota(jnp.int32, (n, LANES), 1)
    o_ref[0, rows, pair * LANES:(pair + 1) * LANES] = jnp.where(lane < HEAD_DIM, o[:n], o[n:]).astype(BF16)


def _values_and_ones(vd):
    return jnp.concatenate([vd, jnp.ones_like(vd)], axis=1)


def _skewed_pairs(scores, softmax, finish, pieces=tuple(range(N_PAIRS))):
    sc, pr = {}, {}
    n = len(pieces)
    for step in range(n + 2):
        if step < n:
            sc[step] = scores(pieces[step])
        if 1 <= step <= n:
            pr[step - 1] = softmax(pieces[step - 1], sc.pop(step - 1))
        if step >= 2:
            finish(pieces[step - 2], pr.pop(step - 2))


def _ga_body(q_ref, kv_ref, o_ref, *, tq):
    blk = min(GA_Q_BLOCK, tq)
    rows = lambda j: slice(j * blk, (j + 1) * blk)

    def scores(piece):
        p, j = piece
        return _dot_nt(_pair_queries(q_ref, p, rows(j)), kv_ref[0, :, (p // 2) * LANES:(p // 2 + 1) * LANES])

    def softmax(piece, sc):
        return jnp.exp2(sc - jnp.max(sc, axis=-1, keepdims=True)).astype(BF16)

    def finish(piece, e):
        p, j = piece
        ov = _dot(e, _values_and_ones(kv_ref[0, :, (2 + p // 2) * LANES:(3 + p // 2) * LANES]))
        _store_pair(o_ref, p, ov[:, :LANES] / ov[:, LANES:], blk, rows(j))

    _skewed_pairs(scores, softmax, finish, [(p, j) for p in range(N_PAIRS) for j in range(tq // blk)])


def _global_attention(q, kv):
    b, s, w = q.shape
    tq = min(GA_Q_TILE, s)
    qs = pl.BlockSpec((1, tq, w), lambda bi, i: (bi, i, 0))
    kvs = pl.BlockSpec((1, s, w), lambda bi, i: (bi, 0, 0))
    return pl.pallas_call(
        functools.partial(_ga_body, tq=tq),
        grid=(b, s // tq),
        in_specs=[qs, kvs],
        out_specs=qs,
        out_shape=jax.ShapeDtypeStruct((b, s, w), BF16),
        compiler_params=_params("parallel", "arbitrary"),
        name="global_attn",
    )(q, kv)


def _wa_body(sink_ref, q_ref, kv_ref, *refs, blk, span, seq, nsub):
    band_refs, o_ref = refs[:nsub], refs[nsub]
    first = pl.program_id(1) * nsub
    starts = [pl.multiple_of(jnp.clip((first + j) * blk - WINDOW, 0, seq - span), WINDOW) for j in range(nsub)]
    valid = []
    for j in range(nsub):
        band = band_refs[j][0] > 0.5
        valid.append(jnp.concatenate([band, band], axis=0))

    def scores(piece):
        p, j = piece
        kd = kv_ref[0, pl.ds(starts[j], span), (p // 2) * LANES:(p // 2 + 1) * LANES]
        sc = _dot_nt(_pair_queries(q_ref, p, slice(j * blk, (j + 1) * blk)), kd)
        return jnp.where(valid[j], sc, NEG_INF)

    def softmax(piece, sc):
        p, _ = piece
        e, sink_e = [], []
        for h in range(2):
            sc_h = sc[h * blk:(h + 1) * blk]
            sink = sink_ref[2 * p + h] * LOG2E
            m = jnp.maximum(jnp.max(sc_h, axis=-1, keepdims=True), sink)
            e.append(jnp.exp2(sc_h - m).astype(BF16))
            sink_e.append(jnp.exp2(sink - m))
        return jnp.concatenate(e, axis=0), jnp.concatenate(sink_e, axis=0)

    def finish(piece, es):
        p, j = piece
        e, sink_e = es
        vd = kv_ref[0, pl.ds(starts[j], span), (2 + p // 2) * LANES:(3 + p // 2) * LANES]
        ov = _dot(e, _values_and_ones(vd))
        _store_pair(o_ref, p, ov[:, :LANES] / (ov[:, LANES:] + sink_e), blk, slice(j * blk, (j + 1) * blk))

    _skewed_pairs(scores, softmax, finish, [(p, j) for p in range(N_PAIRS) for j in range(nsub)])


def _window_attention(q, kv, sink):
    b, s, w = q.shape
    blk = min(WA_Q_BLOCK, s)
    span = min(blk + 2 * WINDOW, s)
    nblk = s // blk
    nsub = WA_SUB_BLOCKS if nblk % WA_SUB_BLOCKS == 0 else 1
    offsets = sorted({int(np.clip(i * blk - WINDOW, 0, s - span)) - i * blk for i in range(nblk)}, reverse=True)
    rel = jnp.arange(span)[None, :] - jnp.arange(blk)[:, None]
    band = jnp.stack([(jnp.abs(rel + off) <= WINDOW).astype(F32) for off in offsets])
    which = lambda i: jnp.where(i == 0, 0, jnp.where(i == nblk - 1, len(offsets) - 1, min(1, len(offsets) - 1)))
    qs = pl.BlockSpec((1, nsub * blk, w), lambda bi, i, *_: (bi, i, 0))
    kvs = pl.BlockSpec((1, s, w), lambda bi, i, *_: (bi, 0, 0))
    bands = [pl.BlockSpec((1, blk, span), functools.partial(lambda j, bi, i, *_: (which(i * nsub + j), 0, 0), j))
             for j in range(nsub)]
    return pl.pallas_call(
        functools.partial(_wa_body, blk=blk, span=span, seq=s, nsub=nsub),
        grid_spec=pltpu.PrefetchScalarGridSpec(
            num_scalar_prefetch=1, grid=(b, nblk // nsub), in_specs=[qs, kvs] + bands, out_specs=qs),
        out_shape=jax.ShapeDtypeStruct((b, s, w), BF16),
        compiler_params=_params("parallel", "arbitrary"),
        name="window_attn",
    )(sink, q, kv, *([band] * nsub))


def _merge_body(x_ref, g_ref, wg_ref, wf_ref, wb_ref, bonus_ref, rgate_ref, lg_ref, lb_ref, ones_ref,
                yb_ref, yc_ref, woa_ref, wob_ref, woc_ref, wout_ref, o_ref):
    ones = ones_ref[...]

    def prepare(rs):
        h = _rmsnorm(x_ref[rs, :], g_ref[...]).astype(BF16)
        wkv = wf_ref[rs, :].astype(F32) + wb_ref[rs, :].astype(F32)
        dev = wkv - _head_sum(wkv, ones, 2) * (1.0 / HEAD_DIM)
        var = _head_sum(dev * dev, ones, 1) * (1.0 / HEAD_DIM)
        y_a = dev * lax.rsqrt(var + LNX_EPS) * lg_ref[...] + lb_ref[...] + bonus_ref[rs, :].astype(F32)
        return h, (y_a * rgate_ref[rs, :].astype(F32)).astype(BF16)

    def gated_sum(rs, h, y_a):
        gates = jax.nn.sigmoid(_dot(h, wg_ref[...]))
        merged = None
        for j, (y, wo_ref) in enumerate(((y_a, woa_ref), (yb_ref[rs, :], wob_ref), (yc_ref[rs, :], woc_ref))):
            term = gates[:, j * D_MODEL:(j + 1) * D_MODEL] * _dot(y, wo_ref[...])
            merged = term if merged is None else merged + term
        return merged.astype(BF16)

    parts = _row_parts(x_ref.shape[0])
    prepared = [prepare(rs) for rs in parts]
    merged = [gated_sum(rs, *pr) for rs, pr in zip(parts, prepared)]
    for rs, m in zip(parts, merged):
        o_ref[rs, :] = x_ref[rs, :] + _dot(m, wout_ref[...])


def _merge(x, g, wg, wkv_f, wkv_b, bonus, rgate, lnx_g, lnx_b, ones_bd, yb, yc, woa, wob, woc, wout):
    t, d = x.shape
    tm = min(TOKEN_TILE, t)
    row = lambda w_: pl.BlockSpec((tm, w_), lambda i: (i, 0))
    half = row(RWKV_DIM)
    return pl.pallas_call(
        _merge_body,
        grid=(t // tm,),
        in_specs=[row(d), _resident((1, d)), _resident(wg.shape), half, half, half, half,
                  _resident(lnx_g.shape), _resident(lnx_b.shape), _resident(ones_bd.shape), half, half,
                  _resident(woa.shape), _resident(wob.shape), _resident(woc.shape), _resident(wout.shape)],
        out_specs=row(d),
        out_shape=jax.ShapeDtypeStruct((t, d), F32),
        compiler_params=_params("parallel"),
        name="merge",
    )(x, g, wg, wkv_f, wkv_b, bonus, rgate, lnx_g, lnx_b, ones_bd, yb, yc, woa, wob, woc, wout)


def _rope_tables(seq):
    t = jnp.arange(seq)

    def angles(pos, dim):
        inv_freq = ROPE_THETA ** (-jnp.arange(0, dim, 2, dtype=F32) / dim)
        return pos.astype(F32)[:, None] * inv_freq[None, :]

    ang_row, ang_col = angles(t // GRID_W, HEAD_DIM // 2), angles(t % GRID_W, HEAD_DIM // 2)
    ang_1d = angles(t, HEAD_DIM)
    ga_cos = jnp.concatenate([jnp.cos(ang_row)] * 2 + [jnp.cos(ang_col)] * 2, axis=1)
    ga_sin = jnp.concatenate([-jnp.sin(ang_row), jnp.sin(ang_row), -jnp.sin(ang_col), jnp.sin(ang_col)], axis=1)
    wa_cos = jnp.concatenate([jnp.cos(ang_1d)] * 2, axis=1)
    wa_sin = jnp.concatenate([-jnp.sin(ang_1d), jnp.sin(ang_1d)], axis=1)
    return tuple(jnp.concatenate([a, a], axis=1) for a in (ga_cos, ga_sin, wa_cos, wa_sin))


def _two_way(w):
    z = jnp.zeros_like(w[0])
    return jnp.concatenate([jnp.concatenate([w[0], z], axis=1), jnp.concatenate([z, w[1]], axis=1)], axis=0)


def kernel(x, ffn1_norm, ffn1_w_up, ffn1_w_down, mix_norm, w_in, rwkv_mu, rwkv_w0, rwkv_w2, rwkv_a0, rwkv_a2, rwkv_g2, rwkv_k_k, rwkv_k_a, rwkv_r_k, rwkv_lnx_g, rwkv_lnx_b, ga_q_norm, ga_k_norm, wa_sink, w_o_rwkv, w_o_ga, w_o_wa, w_out, ffn2_norm, ffn2_w_up, ffn2_w_down, final_norm):
    bsz, seq, dm = x.shape
    depth = w_in.shape[0]
    tokens = bsz * seq
    tabs = _rope_tables(seq)
    idx = jnp.arange(RWKV_DIM)
    ones_bd = (idx[:, None] // HEAD_DIM == idx[None, :] // HEAD_DIM).astype(BF16)
    row1 = lambda a: a.reshape(1, -1)
    final_g = row1(final_norm)

    xt = x.reshape(tokens, dm)
    for l in range(depth):
        xt = _ffn(xt, row1(ffn1_norm[l]), ffn1_w_up[l].astype(BF16), ffn1_w_down[l].astype(BF16), final_g, False)

        w_mix = w_in[l, :, :MIX_COLS].astype(BF16)
        w_gate = w_in[l, :, MIX_COLS:].astype(BF16)
        p_rwkv, ga_q, ga_kv, wa_q, wa_kv = _inproj(
            xt, row1(mix_norm[l]), w_mix, ones_bd, row1(jnp.tile(ga_q_norm[l], 8)), row1(jnp.tile(ga_k_norm[l], 8)),
            tabs, seq)

        outs = _rwkv_prep(
            p_rwkv.reshape(bsz, seq, RWKV_COLS), row1(rwkv_mu[l]), row1(rwkv_w0[l]), _two_way(rwkv_w2[l]).astype(BF16),
            row1(rwkv_a0[l]), _two_way(rwkv_a2[l]).astype(BF16), rwkv_g2[l].astype(BF16), row1(rwkv_k_k[l]),
            row1(rwkv_k_a[l]), row1(rwkv_r_k[l]), ones_bd)
        v16, bonus, gate = outs[16:19]
        y_f, y_b = _rwkv_scan(outs[0:8], outs[8:16], v16)
        flat = lambda a: a.reshape(tokens, RWKV_DIM)

        per_seq = lambda a: a.reshape(bsz, seq, ATT_Q)
        y_g = _global_attention(per_seq(ga_q), per_seq(ga_kv))
        y_w = _window_attention(per_seq(wa_q), per_seq(wa_kv), wa_sink[l])

        xt = _merge(xt, row1(mix_norm[l]), w_gate, flat(y_f), flat(y_b), flat(bonus), flat(gate),
                    row1(rwkv_lnx_g[l]), row1(rwkv_lnx_b[l]), ones_bd, flat(y_g), flat(y_w),
                    w_o_rwkv[l].astype(BF16), w_o_ga[l].astype(BF16), w_o_wa[l].astype(BF16), w_out[l].astype(BF16))

        xt = _ffn(xt, row1(ffn2_norm[l]), ffn2_w_up[l].astype(BF16), ffn2_w_down[l].astype(BF16), final_g,
                  l == depth - 1)
    return xt.reshape(bsz, seq, dm)
```

```python
import functools

import jax
import jax.numpy as jnp
import numpy as np
from jax import lax
from jax.experimental import pallas as pl
from jax.experimental.pallas import tpu as pltpu

F32 = jnp.float32
BF16 = jnp.bfloat16

D_MODEL = 1024
HEAD_DIM = 64
RWKV_DIM = 512
RWKV_COLS = 1920
ATT_Q = 8 * HEAD_DIM
ATT_KV = 2 * HEAD_DIM
GA_COLS = ATT_Q + 2 * ATT_KV
WA_COLS = ATT_Q + 2 * ATT_KV
MIX_COLS = RWKV_COLS + GA_COLS + WA_COLS
D_FF = 2816
GRID_W = 64
WINDOW = 128
ROPE_THETA = 10000.0
NORM_EPS = 1e-6
LNX_EPS = HEAD_DIM * 1e-5
NEG_INF = -1e30
LOG2E = float(np.log2(np.e))
ATTN_SCALE_LOG2 = HEAD_DIM ** -0.5 * LOG2E

CHUNK = 64
QUAD = 4 * HEAD_DIM
LANES = 128
VMEM_LIMIT = 56 * 1024 * 1024

TOKEN_TILE = 1024
ROW_PART = 256
INPROJ_PART = 128
FF_CHUNK = 2816
PREP_TILE = 512
PREP_SUB = 128
SCAN_CHUNKS = 8
SCAN_BATCH = 2
GA_Q_BLOCK = 256
GA_Q_TILE = 1024
WA_Q_BLOCK = 128
WA_SUB_BLOCKS = 8


def _params(*sem):
    return pltpu.CompilerParams(dimension_semantics=sem, vmem_limit_bytes=VMEM_LIMIT)


def _resident(shape):
    nd = len(shape)
    return pl.BlockSpec(shape, lambda *_: (0,) * nd, pipeline_mode=pl.Buffered(1))


def _rmsnorm(x, g):
    return x * lax.rsqrt(jnp.mean(x * x, axis=-1, keepdims=True) + NORM_EPS) * g


def _row_parts(rows, part=None):
    part = min(part or ROW_PART, rows // 2) if rows % 32 == 0 else rows
    return [slice(r, r + part) for r in range(0, rows, part)]


def _dot(a, b):
    return jnp.dot(a, b, preferred_element_type=F32)


def _dot_nt(a, b):
    return lax.dot_general(a, b, (((1,), (1,)), ((), ())), preferred_element_type=F32)


def _dot_tn(a, b):
    return lax.dot_general(a, b, (((0,), (0,)), ((), ())), preferred_element_type=F32)


def _split_bf16(x, terms):
    out = []
    for _ in range(terms - 1):
        hi = x.astype(BF16)
        out.append(hi)
        x = x - hi.astype(F32)
    out.append(x.astype(BF16))
    return out


def _head_sum(x, ones_bd, terms):
    w = x.shape[-1]
    blk = min(w, QUAD)
    ones = ones_bd[:blk, :blk]
    parts = []
    for c0 in range(0, w, blk):
        acc = None
        for t in _split_bf16(x[:, c0:c0 + blk], terms):
            y = _dot(t, ones)
            acc = y if acc is None else acc + y
        parts.append(acc)
    return parts[0] if len(parts) == 1 else jnp.concatenate(parts, axis=1)


def _ffn_body(x_ref, g_ref, wup_ref, wdn_ref, gf_ref, o_ref, act_ref, *, final_norm):
    halves = _row_parts(x_ref.shape[0])
    h = [_rmsnorm(x_ref[rs, :], g_ref[...]).astype(BF16) for rs in halves]
    for c in range(D_FF // FF_CHUNK):
        lo, hi = c * FF_CHUNK, (c + 1) * FF_CHUNK
        for rs, hh in zip(halves, h):
            gate = _dot(hh, wup_ref[:, lo:hi])
            up = _dot(hh, wup_ref[:, D_FF + lo:D_FF + hi])
            act_ref[rs, lo:hi] = (gate * jax.nn.sigmoid(gate) * up).astype(BF16)
    for rs in halves:
        y = x_ref[rs, :] + 0.5 * _dot(act_ref[rs, :], wdn_ref[...])
        if final_norm:
            y = _rmsnorm(y, gf_ref[...])
        o_ref[rs, :] = y


def _ffn(x, g, w_up, w_down, g_final, final_norm):
    t, d = x.shape
    tm = min(TOKEN_TILE, t)
    row = pl.BlockSpec((tm, d), lambda i: (i, 0))
    return pl.pallas_call(
        functools.partial(_ffn_body, final_norm=final_norm),
        grid=(t // tm,),
        in_specs=[row, _resident((1, d)), _resident(w_up.shape), _resident(w_down.shape), _resident((1, d))],
        out_specs=row,
        out_shape=jax.ShapeDtypeStruct((t, d), F32),
        scratch_shapes=[pltpu.VMEM((tm, D_FF), BF16)],
        compiler_params=_params("parallel"),
        name="ffn_final" if final_norm else "ffn",
    )(x, g, w_up, w_down, g_final)


def _rope(t, cos, sin_signed, half):
    w = t.shape[-1]
    lane = lax.broadcasted_iota(jnp.int32, t.shape, 1)
    first = (lane % (2 * half)) < half
    partner = jnp.where(first, pltpu.roll(t, w - half, 1), pltpu.roll(t, half, 1))
    return t * cos + partner * sin_signed


def _dup_heads(t):
    lane = lax.broadcasted_iota(jnp.int32, t.shape, 1)
    swapped = pltpu.roll(t, HEAD_DIM, 1)
    low = lane < HEAD_DIM
    return jnp.where(low, t, swapped), jnp.where(low, swapped, t)


def _inproj_body(x_ref, g_ref, w_ref, ones_ref, gq_ref, gk_ref, cga_ref, sga_ref, cwa_ref, swa_ref,
                 prw_ref, gaq_ref, gakv_ref, waq_ref, wakv_ref):
    ones = ones_ref[...]

    def head_norm(t, gain):
        ms = _head_sum(t * t, ones, 1) * (1.0 / HEAD_DIM)
        return t * lax.rsqrt(ms + NORM_EPS) * gain

    tile4 = lambda tab: jnp.concatenate([tab] * 4, axis=1)

    def project(rs):
        h = _rmsnorm(x_ref[rs, :], g_ref[...]).astype(BF16)
        prw_ref[rs, :] = _dot(h, w_ref[:, :RWKV_COLS])
        return _dot(h, w_ref[:, RWKV_COLS:RWKV_COLS + GA_COLS]), _dot(h, w_ref[:, RWKV_COLS + GA_COLS:MIX_COLS])

    def finish(rs, ga, wa):
        cga, sga = cga_ref[rs, :], sga_ref[rs, :]
        q = _rope(head_norm(ga[:, :ATT_Q], gq_ref[...]), tile4(cga), tile4(sga), 16)
        k = _rope(head_norm(ga[:, ATT_Q:ATT_Q + ATT_KV], gk_ref[:, :ATT_KV]), cga, sga, 16)
        gaq_ref[rs, :] = (q * ATTN_SCALE_LOG2).astype(BF16)
        k0, k1 = _dup_heads(k)
        v0, v1 = _dup_heads(ga[:, ATT_Q + ATT_KV:])
        gakv_ref[rs, :] = jnp.concatenate([k0, k1, v0, v1], axis=1).astype(BF16)
        cwa, swa = cwa_ref[rs, :], swa_ref[rs, :]
        q = _rope(wa[:, :ATT_Q], tile4(cwa), tile4(swa), 32)
        k = _rope(wa[:, ATT_Q:ATT_Q + ATT_KV], cwa, swa, 32)
        waq_ref[rs, :] = (q * ATTN_SCALE_LOG2).astype(BF16)
        k0, k1 = _dup_heads(k)
        v0, v1 = _dup_heads(wa[:, ATT_Q + ATT_KV:])
        wakv_ref[rs, :] = jnp.concatenate([k0, k1, v0, v1], axis=1).astype(BF16)

    parts = _row_parts(x_ref.shape[0], INPROJ_PART)
    projected = {}
    for j in range(len(parts) + 1):
        if j < len(parts):
            projected[j] = project(parts[j])
        if j >= 1:
            finish(parts[j - 1], *projected.pop(j - 1))


def _inproj(x, g, w, ones_bd, gq, gk, tabs, seq):
    t, d = x.shape
    tm = min(TOKEN_TILE, seq)
    per_seq = seq // tm
    row = lambda w_: pl.BlockSpec((tm, w_), lambda i: (i, 0))
    tab = pl.BlockSpec((tm, LANES), lambda i: (i % per_seq, 0))
    return pl.pallas_call(
        _inproj_body,
        grid=(t // tm,),
        in_specs=[row(d), _resident((1, d)), _resident(w.shape), _resident(ones_bd.shape),
                  _resident(gq.shape), _resident(gk.shape), tab, tab, tab, tab],
        out_specs=[row(RWKV_COLS)] + [row(ATT_Q)] * 4,
        out_shape=[jax.ShapeDtypeStruct((t, RWKV_COLS), F32)] + [jax.ShapeDtypeStruct((t, ATT_Q), BF16)] * 4,
        compiler_params=_params("parallel"),
        name="inproj",
    )(x, g, w, ones_bd, gq, gk, *tabs)


def _block_diag4(x):
    head = lax.broadcasted_iota(jnp.int32, x.shape, 1) // HEAD_DIM
    return jnp.concatenate([jnp.where(head == h, x, jnp.zeros_like(x)) for h in range(4)], axis=0)


def _bd16(x):
    return _block_diag4(x).astype(BF16)


def _rwkv_prep_body(p_ref, hp_ref, hn_ref, mu_ref, w0_ref, w2_ref, a0_ref, a2_ref, g2_ref, kk_ref, ka_ref, rk_ref,
                    ones_ref, *out_refs, ts):
    i = pl.program_id(1)
    n = pl.num_programs(1)
    edge_prev = jnp.where(i > 0, hp_ref[0, 7:8, :], 0.0)
    edge_next = jnp.where(i < n - 1, hn_ref[0, 0:1, :], 0.0)
    consts = (mu_ref, w0_ref, w2_ref, a0_ref, a2_ref, g2_ref, kk_ref, ka_ref, rk_ref, ones_ref)
    chain_stage = None
    for r0 in range(0, ts, PREP_SUB):
        prev_row = edge_prev if r0 == 0 else p_ref[0, r0 - 1:r0, :]
        next_row = edge_next if r0 + PREP_SUB == ts else p_ref[0, r0 + PREP_SUB:r0 + PREP_SUB + 1, :]
        state = {}
        vector_stage = _prep_vector_stage(p_ref, prev_row, next_row, consts, out_refs, r0, PREP_SUB, state)
        _alternate(chain_stage, vector_stage)
        chain_stage = _prep_chain_stage(state, out_refs, r0, PREP_SUB)
    _alternate(chain_stage, None)


def _alternate(first, second):
    live = [g for g in (first, second) if g is not None]
    while live:
        for g in list(live):
            try:
                next(g)
            except StopIteration:
                live.remove(g)


def _prep_vector_stage(p_ref, prev_row, next_row, consts, out_refs, r0, ts, state):
    mu_ref, w0_ref, w2_ref, a0_ref, a2_ref, g2_ref, kk_ref, ka_ref, rk_ref, ones_ref = consts
    dir_refs = (out_refs[0:8], out_refs[8:16])
    v_ref, bonus_ref, gate_ref = out_refs[16:19]
    rows = slice(r0, r0 + ts)

    def shifted_mix(lo, hi):
        p = p_ref[0, rows, lo:hi]
        row = lax.broadcasted_iota(jnp.int32, p.shape, 0)
        prev = jnp.where(row == 0, prev_row[:, lo:hi], pltpu.roll(p, 1, 0))
        nxt = jnp.where(row == ts - 1, next_row[:, lo:hi], pltpu.roll(p, ts - 1, 0))
        return p + mu_ref[:, lo:hi] * (0.5 * (prev + nxt) - p)

    lora = shifted_mix(3 * RWKV_DIM, RWKV_COLS)
    wl, al, gl = lora[:, 0:LANES], lora[:, LANES:2 * LANES], lora[:, 2 * LANES:3 * LANES]
    yield
    r = shifted_mix(0, RWKV_DIM)
    yield
    k = shifted_mix(RWKV_DIM, 2 * RWKV_DIM)
    yield
    v = shifted_mix(2 * RWKV_DIM, 3 * RWKV_DIM)
    yield
    ones = ones_ref[...]

    u = w0_ref[...] + _dot(jnp.tanh(wl).astype(BF16), w2_ref[...])
    logw = (-float(np.exp(-0.5))) * jax.nn.sigmoid(u)
    asig = jax.nn.sigmoid(a0_ref[...] + _dot(al.astype(BF16), a2_ref[...]))
    yield
    kkr = k * kk_ref[...]
    kk = kkr * lax.rsqrt(jnp.maximum(_head_sum(kkr * kkr, ones, 1), 1e-24))
    k_a = ka_ref[...]
    kd = [k * (1.0 + (asig[:, d * RWKV_DIM:(d + 1) * RWKV_DIM] - 1.0) * k_a) for d in range(2)]
    yield
    bonus_ref[0, rows] = (_head_sum(r * rk_ref[...] * (kd[0] + kd[1]), ones, 1) * v).astype(BF16)
    gate_ref[0, rows] = _dot(jax.nn.sigmoid(gl).astype(BF16), g2_ref[...]).astype(BF16)
    v_ref[0, rows] = v.astype(BF16)
    yield

    rr = lax.broadcasted_iota(jnp.int32, (ts, ts), 0)
    cc = lax.broadcasted_iota(jnp.int32, (ts, ts), 1)
    same = (rr // CHUNK) == (cc // CHUNK)

    scaled = []
    for d in range(2):
        bp_ref, kp_ref, pt_ref = dir_refs[d][5:8]
        rt_ref = dir_refs[d][1]
        before = (cc <= rr) if d == 0 else (cc >= rr)
        lhs = jnp.where(same & before, 1.0, 0.0).astype(BF16)
        lw = logw[:, d * RWKV_DIM:(d + 1) * RWKV_DIM]
        cl = None
        for term in _split_bf16(lw, 2):
            y = _dot(lhs, term)
            cl = y if cl is None else cl + y
        yield
        ends = [c * CHUNK + (CHUNK - 1 if d == 0 else 0) for c in range(ts // CHUNK)]
        p_tot = [jnp.exp(cl[e:e + 1, :]) for e in ends]
        for c in range(ts // CHUNK):
            pt_ref[0, r0 // CHUNK + c] = p_tot[c]
        e_neg = jnp.exp(-cl)
        b_raw = kk * asig[:, d * RWKV_DIM:(d + 1) * RWKV_DIM]
        r_t = r * jnp.exp(cl)
        rt_ref[0, rows] = r_t.astype(BF16)
        yield
        e_out = e_neg * jnp.concatenate([jnp.broadcast_to(pc, (CHUNK, RWKV_DIM)) for pc in p_tot], axis=0)
        bp_ref[0, rows] = (b_raw * e_out).astype(BF16)
        kp_ref[0, rows] = (kd[d] * e_out).astype(BF16)
        yield
        scaled.append((-kk * jnp.exp(cl - lw), b_raw * e_neg, kd[d] * e_neg, r_t))
        yield
    state.update(scaled=scaled, v=v)


def _prep_chain_stage(state, out_refs, r0, ts):
    scaled, v = state["scaled"], state["v"]
    dir_refs = (out_refs[0:8], out_refs[8:16])
    tt = lax.broadcasted_iota(jnp.int32, (CHUNK, QUAD), 0)
    ss = lax.broadcasted_iota(jnp.int32, (CHUNK, QUAD), 1) % CHUNK
    eye = jnp.where(tt == ss, 1.0, 0.0)
    chains = [(d, c, qd) for d in range(2) for c in range(ts // CHUNK) for qd in range(RWKV_DIM // QUAD)]
    cut = lambda arr, c, qd: arr[c * CHUNK:(c + 1) * CHUNK, qd * QUAD:(qd + 1) * QUAD]
    strict = [ss < tt, ss > tt]
    incl = [ss <= tt, ss >= tt]
    at, x2 = {}, {}
    for ch in chains:
        d, c, qd = ch
        at[ch] = cut(scaled[d][0], c, qd)
        x2[ch] = jnp.concatenate([at[ch], cut(scaled[d][3], c, qd)], axis=0).astype(BF16)
    w_b = {ch: _dot_nt(x2[ch], _bd16(cut(scaled[ch[0]][1], ch[1], ch[2]))) for ch in chains}
    yield
    w_k = {ch: _dot_nt(x2[ch], _bd16(cut(scaled[ch[0]][2], ch[1], ch[2]))) for ch in chains}
    yield
    a_ab, gy = {}, {}
    for ch in chains:
        d, c, qd = ch
        a_ab[ch] = jnp.where(strict[d], w_b[ch][:CHUNK], 0.0)
        dir_refs[d][2][0, r0 + c * CHUNK:r0 + (c + 1) * CHUNK, qd * QUAD:(qd + 1) * QUAD] = (
            jnp.where(incl[d], w_b[ch][CHUNK:], 0.0).astype(BF16))
        lk = jnp.concatenate([jnp.where(strict[d], w_k[ch][:CHUNK], 0.0),
                              jnp.where(incl[d], w_k[ch][CHUNK:], 0.0)], axis=0)
        gy[ch] = _dot(lk.astype(BF16), _bd16(cut(v, c, qd)))
    yield
    tm = {ch: eye + a_ab[ch] for ch in chains}
    pw = {ch: _dot(a_ab[ch].astype(BF16), _bd16(a_ab[ch])) for ch in chains}
    yield
    for _ in range(4):
        both = {ch: _dot(jnp.concatenate([pw[ch], tm[ch]], axis=0).astype(BF16), _bd16(pw[ch])) for ch in chains}
        tm = {ch: tm[ch] + both[ch][CHUNK:] for ch in chains}
        pw = {ch: both[ch][:CHUNK] for ch in chains}
        yield
    tm = {ch: (tm[ch] + _dot(tm[ch].astype(BF16), _bd16(pw[ch]))).astype(BF16) for ch in chains}
    yield
    for ch in chains:
        d, c, qd = ch
        rs, ls = slice(r0 + c * CHUNK, r0 + (c + 1) * CHUNK), slice(qd * QUAD, (qd + 1) * QUAD)
        at_ref, _, _, uv_ref, yv_ref = dir_refs[d][0:5]
        uv_ref[0, rs, ls] = _dot(tm[ch], _bd16(gy[ch][:CHUNK])).astype(BF16)
        at_ref[0, rs, ls] = _dot(tm[ch], _bd16(at[ch])).astype(BF16)
        yv_ref[0, rs, ls] = gy[ch][CHUNK:].astype(BF16)


def _rwkv_prep(p, mu, w0, w2bd, a0, a2bd, g2, k_k, k_a, r_k, ones_bd):
    b, s, _ = p.shape
    ts = min(PREP_TILE, s)
    nt = s // ts
    h8 = ts // 8
    main = pl.BlockSpec((1, ts, RWKV_COLS), lambda bi, i: (bi, i, 0))
    halo_prev = pl.BlockSpec((1, 8, RWKV_COLS), lambda bi, i: (bi, jnp.maximum(i * h8 - 1, 0), 0))
    halo_next = pl.BlockSpec((1, 8, RWKV_COLS), lambda bi, i: (bi, jnp.minimum((i + 1) * h8, s // 8 - 1), 0))
    tok = pl.BlockSpec((1, ts, RWKV_DIM), lambda bi, i: (bi, i, 0))
    ptot = pl.BlockSpec((1, ts // CHUNK, 1, RWKV_DIM), lambda bi, i: (bi, i, 0, 0))
    tok16 = jax.ShapeDtypeStruct((b, s, RWKV_DIM), BF16)
    tok32 = jax.ShapeDtypeStruct((b, s, RWKV_DIM), F32)
    pt32 = jax.ShapeDtypeStruct((b, s // CHUNK, 1, RWKV_DIM), F32)
    per_dir_specs = [tok, tok, tok, tok, tok, tok, tok, ptot]
    per_dir_shapes = [tok16] * 7 + [pt32]
    consts = [mu, w0, w2bd, a0, a2bd, g2, k_k, k_a, r_k, ones_bd]
    return pl.pallas_call(
        functools.partial(_rwkv_prep_body, ts=ts),
        grid=(b, nt),
        in_specs=[main, halo_prev, halo_next] + [_resident(c.shape) for c in consts],
        out_specs=per_dir_specs * 2 + [tok, tok, tok],
        out_shape=per_dir_shapes * 2 + [tok16, tok16, tok16],
        compiler_params=_params("parallel", "parallel"),
        name="rwkv_prep",
    )(p, p, p, *consts)


def _rwkv_scan_body(*refs, nc, nb):
    fwd, bwd = refs[0:8], refs[8:16]
    vf_ref, vb_ref = refs[16:18]
    yf_ref, yb_ref = refs[18:20]
    st_ref = refs[20]
    step = pl.program_id(1)

    @pl.when(step == 0)
    def _():
        st_ref[...] = jnp.zeros_like(st_ref)

    rr = lax.broadcasted_iota(jnp.int32, (QUAD, QUAD), 0) // HEAD_DIM
    cc = lax.broadcasted_iota(jnp.int32, (QUAD, QUAD), 1) // HEAD_DIM
    diag = rr == cc
    dirs = ((fwd, vf_ref, yf_ref), (bwd, vb_ref, yb_ref))
    chains = [(bi, d, qd) for bi in range(nb) for d in range(2) for qd in range(RWKV_DIM // QUAD)]
    st = {ch: st_ref[ch] for ch in chains}
    for j in range(nc):
        z = {}
        for ch in chains:
            bi, d, qd = ch
            c = j if d == 0 else nc - 1 - j
            rs, ls = slice(c * CHUNK, (c + 1) * CHUNK), slice(qd * QUAD, (qd + 1) * QUAD)
            at_ref, rt_ref = dirs[d][0][0:2]
            x2 = jnp.concatenate([at_ref[bi, rs, ls], rt_ref[bi, rs, ls]], axis=0)
            z[ch] = _dot_nt(x2, st[ch].astype(BF16))
        for ch in chains:
            bi, d, qd = ch
            c = j if d == 0 else nc - 1 - j
            rs, ls = slice(c * CHUNK, (c + 1) * CHUNK), slice(qd * QUAD, (qd + 1) * QUAD)
            (_, _, mrb_ref, uv_ref, yv_ref, bp_ref, kp_ref, pt_ref), v_ref, y_ref = dirs[d]
            u = z[ch][:CHUNK] + uv_ref[bi, rs, ls]
            y = z[ch][CHUNK:] + _dot(mrb_ref[bi, rs, ls], _bd16(u)) + yv_ref[bi, rs, ls]
            y_ref[bi, rs, ls] = y.astype(BF16)
            uv2 = jnp.concatenate([u.astype(BF16), v_ref[bi, rs, ls]], axis=0)
            bk2 = jnp.concatenate([bp_ref[bi, rs, ls], kp_ref[bi, rs, ls]], axis=0)
            st[ch] = st[ch] * pt_ref[bi, c][:, ls] + jnp.where(diag, _dot_tn(uv2, bk2), 0.0)
    for ch in chains:
        st_ref[ch] = st[ch]


def _rwkv_scan(fwd_arrs, bwd_arrs, v16):
    b, s, _ = v16.shape
    nc = min(SCAN_CHUNKS, s // CHUNK)
    nb = SCAN_BATCH if b % SCAN_BATCH == 0 else 1
    rows = nc * CHUNK
    steps = s // rows
    f_tok = pl.BlockSpec((nb, rows, RWKV_DIM), lambda bi, i: (bi, i, 0))
    b_tok = pl.BlockSpec((nb, rows, RWKV_DIM), lambda bi, i: (bi, steps - 1 - i, 0))
    f_pt = pl.BlockSpec((nb, nc, 1, RWKV_DIM), lambda bi, i: (bi, i, 0, 0))
    b_pt = pl.BlockSpec((nb, nc, 1, RWKV_DIM), lambda bi, i: (bi, steps - 1 - i, 0, 0))
    y_shape = jax.ShapeDtypeStruct((b, s, RWKV_DIM), BF16)
    return pl.pallas_call(
        functools.partial(_rwkv_scan_body, nc=nc, nb=nb),
        grid=(b // nb, steps),
        in_specs=[f_tok] * 7 + [f_pt] + [b_tok] * 7 + [b_pt] + [f_tok, b_tok],
        out_specs=[f_tok, b_tok],
        out_shape=[y_shape, y_shape],
        scratch_shapes=[pltpu.VMEM((nb, 2, RWKV_DIM // QUAD, QUAD, QUAD), F32)],
        compiler_params=_params("parallel", "arbitrary"),
        name="rwkv_scan",
    )(*fwd_arrs, *bwd_arrs, v16, v16)


N_PAIRS = 4


def _pair_queries(q_ref, pair, rows=slice(None)):
    qp = q_ref[0, rows, pair * LANES:(pair + 1) * LANES]
    lane = lax.broadcasted_iota(jnp.int32, qp.shape, 1)
    zero = jnp.zeros_like(qp)
    return jnp.concatenate([jnp.where(lane < HEAD_DIM, qp, zero), jnp.where(lane >= HEAD_DIM, qp, zero)], axis=0)


def _store_pair(o_ref, pair, o, n, rows=slice(None)):
    lane = lax.broadcasted_iota(jnp.int32, (n, LANES), 1)
    o_ref[0, rows, pair * LANES:(pair + 1) * LANES] = jnp.where(lane < HEAD_DIM, o[:n], o[n:]).astype(BF16)


def _values_and_ones(vd):
    return jnp.concatenate([vd, jnp.ones_like(vd)], axis=1)


def _skewed_pairs(scores, softmax, finish, pieces=tuple(range(N_PAIRS))):
    sc, pr = {}, {}
    n = len(pieces)
    for step in range(n + 2):
        if step < n:
            sc[step] = scores(pieces[step])
        if 1 <= step <= n:
            pr[step - 1] = softmax(pieces[step - 1], sc.pop(step - 1))
        if step >= 2:
            finish(pieces[step - 2], pr.pop(step - 2))


def _ga_body(q_ref, kv_ref, o_ref, *, tq):
    blk = min(GA_Q_BLOCK, tq)
    rows = lambda j: slice(j * blk, (j + 1) * blk)

    def scores(piece):
        p, j = piece
        return _dot_nt(_pair_queries(q_ref, p, rows(j)), kv_ref[0, :, (p // 2) * LANES:(p // 2 + 1) * LANES])

    def softmax(piece, sc):
        return jnp.exp2(sc - jnp.max(sc, axis=-1, keepdims=True)).astype(BF16)

    def finish(piece, e):
        p, j = piece
        ov = _dot(e, _values_and_ones(kv_ref[0, :, (2 + p // 2) * LANES:(3 + p // 2) * LANES]))
        _store_pair(o_ref, p, ov[:, :LANES] / ov[:, LANES:], blk, rows(j))

    _skewed_pairs(scores, softmax, finish, [(p, j) for p in range(N_PAIRS) for j in range(tq // blk)])


def _global_attention(q, kv):
    b, s, w = q.shape
    tq = min(GA_Q_TILE, s)
    qs = pl.BlockSpec((1, tq, w), lambda bi, i: (bi, i, 0))
    kvs = pl.BlockSpec((1, s, w), lambda bi, i: (bi, 0, 0))
    return pl.pallas_call(
        functools.partial(_ga_body, tq=tq),
        grid=(b, s // tq),
        in_specs=[qs, kvs],
        out_specs=qs,
        out_shape=jax.ShapeDtypeStruct((b, s, w), BF16),
        compiler_params=_params("parallel", "arbitrary"),
        name="global_attn",
    )(q, kv)


def _wa_body(sink_ref, q_ref, kv_ref, *refs, blk, span, seq, nsub):
    band_refs, o_ref = refs[:nsub], refs[nsub]
    first = pl.program_id(1) * nsub
    starts = [pl.multiple_of(jnp.clip((first + j) * blk - WINDOW, 0, seq - span), WINDOW) for j in range(nsub)]
    valid = []
    for j in range(nsub):
        band = band_refs[j][0] > 0.5
        valid.append(jnp.concatenate([band, band], axis=0))

    def scores(piece):
        p, j = piece
        kd = kv_ref[0, pl.ds(starts[j], span), (p // 2) * LANES:(p // 2 + 1) * LANES]
        sc = _dot_nt(_pair_queries(q_ref, p, slice(j * blk, (j + 1) * blk)), kd)
        return jnp.where(valid[j], sc, NEG_INF)

    def softmax(piece, sc):
        p, _ = piece
        e, sink_e = [], []
        for h in range(2):
            sc_h = sc[h * blk:(h + 1) * blk]
            sink = sink_ref[2 * p + h] * LOG2E
            m = jnp.maximum(jnp.max(sc_h, axis=-1, keepdims=True), sink)
            e.append(jnp.exp2(sc_h - m).astype(BF16))
            sink_e.append(jnp.exp2(sink - m))
        return jnp.concatenate(e, axis=0), jnp.concatenate(sink_e, axis=0)

    def finish(piece, es):
        p, j = piece
        e, sink_e = es
        vd = kv_ref[0, pl.ds(starts[j], span), (2 + p // 2) * LANES:(3 + p // 2) * LANES]
        ov = _dot(e, _values_and_ones(vd))
        _store_pair(o_ref, p, ov[:, :LANES] / (ov[:, LANES:] + sink_e), blk, slice(j * blk, (j + 1) * blk))

    _skewed_pairs(scores, softmax, finish, [(p, j) for p in range(N_PAIRS) for j in range(nsub)])


def _window_attention(q, kv, sink):
    b, s, w = q.shape
    blk = min(WA_Q_BLOCK, s)
    span = min(blk + 2 * WINDOW, s)
    nblk = s // blk
    nsub = WA_SUB_BLOCKS if nblk % WA_SUB_BLOCKS == 0 else 1
    offsets = sorted({int(np.clip(i * blk - WINDOW, 0, s - span)) - i * blk for i in range(nblk)}, reverse=True)
    rel = jnp.arange(span)[None, :] - jnp.arange(blk)[:, None]
    band = jnp.stack([(jnp.abs(rel + off) <= WINDOW).astype(F32) for off in offsets])
    which = lambda i: jnp.where(i == 0, 0, jnp.where(i == nblk - 1, len(offsets) - 1, min(1, len(offsets) - 1)))
    qs = pl.BlockSpec((1, nsub * blk, w), lambda bi, i, *_: (bi, i, 0))
    kvs = pl.BlockSpec((1, s, w), lambda bi, i, *_: (bi, 0, 0))
    bands = [pl.BlockSpec((1, blk, span), functools.partial(lambda j, bi, i, *_: (which(i * nsub + j), 0, 0), j))
             for j in range(nsub)]
    return pl.pallas_call(
        functools.partial(_wa_body, blk=blk, span=span, seq=s, nsub=nsub),
        grid_spec=pltpu.PrefetchScalarGridSpec(
            num_scalar_prefetch=1, grid=(b, nblk // nsub), in_specs=[qs, kvs] + bands, out_specs=qs),
        out_shape=jax.ShapeDtypeStruct((b, s, w), BF16),
        compiler_params=_params("parallel", "arbitrary"),
        name="window_attn",
    )(sink, q, kv, *([band] * nsub))


def _merge_body(x_ref, g_ref, wg_ref, wf_ref, wb_ref, bonus_ref, rgate_ref, lg_ref, lb_ref, ones_ref,
                yb_ref, yc_ref, woa_ref, wob_ref, woc_ref, wout_ref, o_ref):
    ones = ones_ref[...]

    def prepare(rs):
        h = _rmsnorm(x_ref[rs, :], g_ref[...]).astype(BF16)
        wkv = wf_ref[rs, :].astype(F32) + wb_ref[rs, :].astype(F32)
        dev = wkv - _head_sum(wkv, ones, 2) * (1.0 / HEAD_DIM)
        var = _head_sum(dev * dev, ones, 1) * (1.0 / HEAD_DIM)
        y_a = dev * lax.rsqrt(var + LNX_EPS) * lg_ref[...] + lb_ref[...] + bonus_ref[rs, :].astype(F32)
        return h, (y_a * rgate_ref[rs, :].astype(F32)).astype(BF16)

    def gated_sum(rs, h, y_a):
        gates = jax.nn.sigmoid(_dot(h, wg_ref[...]))
        merged = None
        for j, (y, wo_ref) in enumerate(((y_a, woa_ref), (yb_ref[rs, :], wob_ref), (yc_ref[rs, :], woc_ref))):
            term = gates[:, j * D_MODEL:(j + 1) * D_MODEL] * _dot(y, wo_ref[...])
            merged = term if merged is None else merged + term
        return merged.astype(BF16)

    parts = _row_parts(x_ref.shape[0])
    prepared = [prepare(rs) for rs in parts]
    merged = [gated_sum(rs, *pr) for rs, pr in zip(parts, prepared)]
    for rs, m in zip(parts, merged):
        o_ref[rs, :] = x_ref[rs, :] + _dot(m, wout_ref[...])


def _merge(x, g, wg, wkv_f, wkv_b, bonus, rgate, lnx_g, lnx_b, ones_bd, yb, yc, woa, wob, woc, wout):
    t, d = x.shape
    tm = min(TOKEN_TILE, t)
    row = lambda w_: pl.BlockSpec((tm, w_), lambda i: (i, 0))
    half = row(RWKV_DIM)
    return pl.pallas_call(
        _merge_body,
        grid=(t // tm,),
        in_specs=[row(d), _resident((1, d)), _resident(wg.shape), half, half, half, half,
                  _resident(lnx_g.shape), _resident(lnx_b.shape), _resident(ones_bd.shape), half, half,
                  _resident(woa.shape), _resident(wob.shape), _resident(woc.shape), _resident(wout.shape)],
        out_specs=row(d),
        out_shape=jax.ShapeDtypeStruct((t, d), F32),
        compiler_params=_params("parallel"),
        name="merge",
    )(x, g, wg, wkv_f, wkv_b, bonus, rgate, lnx_g, lnx_b, ones_bd, yb, yc, woa, wob, woc, wout)


def _rope_tables(seq):
    t = jnp.arange(seq)

    def angles(pos, dim):
        inv_freq = ROPE_THETA ** (-jnp.arange(0, dim, 2, dtype=F32) / dim)
        return pos.astype(F32)[:, None] * inv_freq[None, :]

    ang_row, ang_col = angles(t // GRID_W, HEAD_DIM // 2), angles(t % GRID_W, HEAD_DIM // 2)
    ang_1d = angles(t, HEAD_DIM)
    ga_cos = jnp.concatenate([jnp.cos(ang_row)] * 2 + [jnp.cos(ang_col)] * 2, axis=1)
    ga_sin = jnp.concatenate([-jnp.sin(ang_row), jnp.sin(ang_row), -jnp.sin(ang_col), jnp.sin(ang_col)], axis=1)
    wa_cos = jnp.concatenate([jnp.cos(ang_1d)] * 2, axis=1)
    wa_sin = jnp.concatenate([-jnp.sin(ang_1d), jnp.sin(ang_1d)], axis=1)
    return tuple(jnp.concatenate([a, a], axis=1) for a in (ga_cos, ga_sin, wa_cos, wa_sin))


def _two_way(w):
    z = jnp.zeros_like(w[0])
    return jnp.concatenate([jnp.concatenate([w[0], z], axis=1), jnp.concatenate([z, w[1]], axis=1)], axis=0)


def kernel(x, ffn1_norm, ffn1_w_up, ffn1_w_down, mix_norm, w_in, rwkv_mu, rwkv_w0, rwkv_w2, rwkv_a0, rwkv_a2, rwkv_g2, rwkv_k_k, rwkv_k_a, rwkv_r_k, rwkv_lnx_g, rwkv_lnx_b, ga_q_norm, ga_k_norm, wa_sink, w_o_rwkv, w_o_ga, w_o_wa, w_out, ffn2_norm, ffn2_w_up, ffn2_w_down, final_norm):
    bsz, seq, dm = x.shape
    depth = w_in.shape[0]
    tokens = bsz * seq
    tabs = _rope_tables(seq)
    idx = jnp.arange(RWKV_DIM)
    ones_bd = (idx[:, None] // HEAD_DIM == idx[None, :] // HEAD_DIM).astype(BF16)
    row1 = lambda a: a.reshape(1, -1)
    final_g = row1(final_norm)

    xt = x.reshape(tokens, dm)
    for l in range(depth):
        xt = _ffn(xt, row1(ffn1_norm[l]), ffn1_w_up[l].astype(BF16), ffn1_w_down[l].astype(BF16), final_g, False)

        w_mix = w_in[l, :, :MIX_COLS].astype(BF16)
        w_gate = w_in[l, :, MIX_COLS:].astype(BF16)
        p_rwkv, ga_q, ga_kv, wa_q, wa_kv = _inproj(
            xt, row1(mix_norm[l]), w_mix, ones_bd, row1(jnp.tile(ga_q_norm[l], 8)), row1(jnp.tile(ga_k_norm[l], 8)),
            tabs, seq)

        outs = _rwkv_prep(
            p_rwkv.reshape(bsz, seq, RWKV_COLS), row1(rwkv_mu[l]), row1(rwkv_w0[l]), _two_way(rwkv_w2[l]).astype(BF16),
            row1(rwkv_a0[l]), _two_way(rwkv_a2[l]).astype(BF16), rwkv_g2[l].astype(BF16), row1(rwkv_k_k[l]),
            row1(rwkv_k_a[l]), row1(rwkv_r_k[l]), ones_bd)
        v16, bonus, gate = outs[16:19]
        y_f, y_b = _rwkv_scan(outs[0:8], outs[8:16], v16)
        flat = lambda a: a.reshape(tokens, RWKV_DIM)

        per_seq = lambda a: a.reshape(bsz, seq, ATT_Q)
        y_g = _global_attention(per_seq(ga_q), per_seq(ga_kv))
        y_w = _window_attention(per_seq(wa_q), per_seq(wa_kv), wa_sink[l])

        xt = _merge(xt, row1(mix_norm[l]), w_gate, flat(y_f), flat(y_b), flat(bonus), flat(gate),
                    row1(rwkv_lnx_g[l]), row1(rwkv_lnx_b[l]), ones_bd, flat(y_g), flat(y_w),
                    w_o_rwkv[l].astype(BF16), w_o_ga[l].astype(BF16), w_o_wa[l].astype(BF16), w_out[l].astype(BF16))

        xt = _ffn(xt, row1(ffn2_norm[l]), ffn2_w_up[l].astype(BF16), ffn2_w_down[l].astype(BF16), final_g,
                  l == depth - 1)
    return xt.reshape(bsz, seq, dm)
```

```python
import functools

import jax
import jax.numpy as jnp
import numpy as np
from jax import lax
from jax.experimental import pallas as pl
from jax.experimental.pallas import tpu as pltpu

F32 = jnp.float32
BF16 = jnp.bfloat16

D_MODEL = 1024
HEAD_DIM = 64
RWKV_DIM = 512
RWKV_COLS = 1920
ATT_Q = 8 * HEAD_DIM
ATT_KV = 2 * HEAD_DIM
GA_COLS = ATT_Q + 2 * ATT_KV
WA_COLS = ATT_Q + 2 * ATT_KV
MIX_COLS = RWKV_COLS + GA_COLS + WA_COLS
D_FF = 2816
GRID_W = 64
WINDOW = 128
ROPE_THETA = 10000.0
NORM_EPS = 1e-6
LNX_EPS = HEAD_DIM * 1e-5
NEG_INF = -1e30
LOG2E = float(np.log2(np.e))
ATTN_SCALE_LOG2 = HEAD_DIM ** -0.5 * LOG2E

CHUNK = 64
QUAD = 4 * HEAD_DIM
LANES = 128
VMEM_LIMIT = 56 * 1024 * 1024

TOKEN_TILE = 1024
CAST_ROWS = 256
ROW_PART = 256
INPROJ_PART = 128
FF_CHUNK = 2816
PREP_TILE = 512
PREP_SUB = 128
SCAN_CHUNKS = 8
SCAN_BATCH = 2
GA_Q_BLOCK = 256
GA_Q_TILE = 1024
WA_Q_BLOCK = 128
WA_SUB_BLOCKS = 8


def _params(*sem):
    return pltpu.CompilerParams(dimension_semantics=sem, vmem_limit_bytes=VMEM_LIMIT)


def _resident(shape):
    nd = len(shape)
    return pl.BlockSpec(shape, lambda *_: (0,) * nd, pipeline_mode=pl.Buffered(1))


def _layer_weight(stack, layer):
    spec = pl.BlockSpec((None,) + stack.shape[1:], lambda *_: (layer, 0, 0), pipeline_mode=pl.Buffered(1))
    return spec, stack


def _cast_body(w_ref, o_ref):
    o_ref[...] = w_ref[...].astype(BF16)


def _to_bf16(w):
    nl, r, c = w.shape
    tr = CAST_ROWS if r % CAST_ROWS == 0 else r
    spec = pl.BlockSpec((1, tr, c), lambda li, i: (li, i, 0))
    return pl.pallas_call(
        _cast_body,
        grid=(nl, r // tr),
        in_specs=[spec],
        out_specs=spec,
        out_shape=jax.ShapeDtypeStruct(w.shape, BF16),
        compiler_params=_params("parallel", "parallel"),
        name="cast_weights",
    )(w)


def _rmsnorm(x, g):
    return x * lax.rsqrt(jnp.mean(x * x, axis=-1, keepdims=True) + NORM_EPS) * g


def _row_parts(rows, part=None):
    part = min(part or ROW_PART, rows // 2) if rows % 32 == 0 else rows
    return [slice(r, r + part) for r in range(0, rows, part)]


def _dot(a, b):
    return jnp.dot(a, b, preferred_element_type=F32)


def _dot_nt(a, b):
    return lax.dot_general(a, b, (((1,), (1,)), ((), ())), preferred_element_type=F32)


def _dot_tn(a, b):
    return lax.dot_general(a, b, (((0,), (0,)), ((), ())), preferred_element_type=F32)


def _split_bf16(x, terms):
    out = []
    for _ in range(terms - 1):
        hi = x.astype(BF16)
        out.append(hi)
        x = x - hi.astype(F32)
    out.append(x.astype(BF16))
    return out


def _head_sum(x, ones_bd, terms):
    w = x.shape[-1]
    blk = min(w, QUAD)
    ones = ones_bd[:blk, :blk]
    parts = []
    for c0 in range(0, w, blk):
        acc = None
        for t in _split_bf16(x[:, c0:c0 + blk], terms):
            y = _dot(t, ones)
            acc = y if acc is None else acc + y
        parts.append(acc)
    return parts[0] if len(parts) == 1 else jnp.concatenate(parts, axis=1)


def _ffn_body(x_ref, g_ref, wup_ref, wdn_ref, gf_ref, o_ref, act_ref, *, final_norm):
    halves = _row_parts(x_ref.shape[0])
    h = [_rmsnorm(x_ref[rs, :], g_ref[...]).astype(BF16) for rs in halves]
    for c in range(D_FF // FF_CHUNK):
        lo, hi = c * FF_CHUNK, (c + 1) * FF_CHUNK
        for rs, hh in zip(halves, h):
            gate = _dot(hh, wup_ref[:, lo:hi])
            up = _dot(hh, wup_ref[:, D_FF + lo:D_FF + hi])
            act_ref[rs, lo:hi] = (gate * jax.nn.sigmoid(gate) * up).astype(BF16)
    for rs in halves:
        y = x_ref[rs, :] + 0.5 * _dot(act_ref[rs, :], wdn_ref[...])
        if final_norm:
            y = _rmsnorm(y, gf_ref[...])
        o_ref[rs, :] = y


def _ffn(x, g, w_up, w_down, layer, g_final, final_norm):
    t, d = x.shape
    tm = min(TOKEN_TILE, t)
    row = pl.BlockSpec((tm, d), lambda i: (i, 0))
    (up_spec, w_up), (down_spec, w_down) = _layer_weight(w_up, layer), _layer_weight(w_down, layer)
    return pl.pallas_call(
        functools.partial(_ffn_body, final_norm=final_norm),
        grid=(t // tm,),
        in_specs=[row, _resident((1, d)), up_spec, down_spec, _resident((1, d))],
        out_specs=row,
        out_shape=jax.ShapeDtypeStruct((t, d), F32),
        scratch_shapes=[pltpu.VMEM((tm, D_FF), BF16)],
        compiler_params=_params("parallel"),
        name="ffn_final" if final_norm else "ffn",
    )(x, g, w_up, w_down, g_final)


def _rope(t, cos, sin_signed, half):
    w = t.shape[-1]
    lane = lax.broadcasted_iota(jnp.int32, t.shape, 1)
    first = (lane % (2 * half)) < half
    partner = jnp.where(first, pltpu.roll(t, w - half, 1), pltpu.roll(t, half, 1))
    return t * cos + partner * sin_signed


def _dup_heads(t):
    lane = lax.broadcasted_iota(jnp.int32, t.shape, 1)
    swapped = pltpu.roll(t, HEAD_DIM, 1)
    low = lane < HEAD_DIM
    return jnp.where(low, t, swapped), jnp.where(low, swapped, t)


def _inproj_body(x_ref, g_ref, w_ref, ones_ref, gq_ref, gk_ref, cga_ref, sga_ref, cwa_ref, swa_ref,
                 prw_ref, gaq_ref, gakv_ref, waq_ref, wakv_ref):
    ones = ones_ref[...]

    def head_norm(t, gain):
        ms = _head_sum(t * t, ones, 1) * (1.0 / HEAD_DIM)
        return t * lax.rsqrt(ms + NORM_EPS) * gain

    tile4 = lambda tab: jnp.concatenate([tab] * 4, axis=1)

    def project(rs):
        h = _rmsnorm(x_ref[rs, :], g_ref[...]).astype(BF16)
        prw_ref[rs, :] = _dot(h, w_ref[:, :RWKV_COLS])
        return _dot(h, w_ref[:, RWKV_COLS:RWKV_COLS + GA_COLS]), _dot(h, w_ref[:, RWKV_COLS + GA_COLS:MIX_COLS])

    def finish(rs, ga, wa):
        cga, sga = cga_ref[rs, :], sga_ref[rs, :]
        q = _rope(head_norm(ga[:, :ATT_Q], gq_ref[...]), tile4(cga), tile4(sga), 16)
        k = _rope(head_norm(ga[:, ATT_Q:ATT_Q + ATT_KV], gk_ref[:, :ATT_KV]), cga, sga, 16)
        gaq_ref[rs, :] = (q * ATTN_SCALE_LOG2).astype(BF16)
        k0, k1 = _dup_heads(k)
        v0, v1 = _dup_heads(ga[:, ATT_Q + ATT_KV:])
        gakv_ref[rs, :] = jnp.concatenate([k0, k1, v0, v1], axis=1).astype(BF16)
        cwa, swa = cwa_ref[rs, :], swa_ref[rs, :]
        q = _rope(wa[:, :ATT_Q], tile4(cwa), tile4(swa), 32)
        k = _rope(wa[:, ATT_Q:ATT_Q + ATT_KV], cwa, swa, 32)
        waq_ref[rs, :] = (q * ATTN_SCALE_LOG2).astype(BF16)
        k0, k1 = _dup_heads(k)
        v0, v1 = _dup_heads(wa[:, ATT_Q + ATT_KV:])
        wakv_ref[rs, :] = jnp.concatenate([k0, k1, v0, v1], axis=1).astype(BF16)

    parts = _row_parts(x_ref.shape[0], INPROJ_PART)
    projected = {}
    for j in range(len(parts) + 1):
        if j < len(parts):
            projected[j] = project(parts[j])
        if j >= 1:
            finish(parts[j - 1], *projected.pop(j - 1))


def _inproj(x, g, w, layer, ones_bd, gq, gk, tabs, seq):
    t, d = x.shape
    tm = min(TOKEN_TILE, seq)
    per_seq = seq // tm
    row = lambda w_: pl.BlockSpec((tm, w_), lambda i: (i, 0))
    tab = pl.BlockSpec((tm, LANES), lambda i: (i % per_seq, 0))
    w_spec, w = _layer_weight(w, layer)
    return pl.pallas_call(
        _inproj_body,
        grid=(t // tm,),
        in_specs=[row(d), _resident((1, d)), w_spec, _resident(ones_bd.shape),
                  _resident(gq.shape), _resident(gk.shape), tab, tab, tab, tab],
        out_specs=[row(RWKV_COLS)] + [row(ATT_Q)] * 4,
        out_shape=[jax.ShapeDtypeStruct((t, RWKV_COLS), F32)] + [jax.ShapeDtypeStruct((t, ATT_Q), BF16)] * 4,
        compiler_params=_params("parallel"),
        name="inproj",
    )(x, g, w, ones_bd, gq, gk, *tabs)


def _block_diag4(x):
    head = lax.broadcasted_iota(jnp.int32, x.shape, 1) // HEAD_DIM
    return jnp.concatenate([jnp.where(head == h, x, jnp.zeros_like(x)) for h in range(4)], axis=0)


def _bd16(x):
    return _block_diag4(x).astype(BF16)


def _rwkv_prep_body(p_ref, hp_ref, hn_ref, mu_ref, w0_ref, w2_ref, a0_ref, a2_ref, g2_ref, kk_ref, ka_ref, rk_ref,
                    ones_ref, *out_refs, ts):
    i = pl.program_id(1)
    n = pl.num_programs(1)
    edge_prev = jnp.where(i > 0, hp_ref[0, 7:8, :], 0.0)
    edge_next = jnp.where(i < n - 1, hn_ref[0, 0:1, :], 0.0)
    consts = (mu_ref, w0_ref, w2_ref, a0_ref, a2_ref, g2_ref, kk_ref, ka_ref, rk_ref, ones_ref)
    chain_stage = None
    for r0 in range(0, ts, PREP_SUB):
        prev_row = edge_prev if r0 == 0 else p_ref[0, r0 - 1:r0, :]
        next_row = edge_next if r0 + PREP_SUB == ts else p_ref[0, r0 + PREP_SUB:r0 + PREP_SUB + 1, :]
        state = {}
        vector_stage = _prep_vector_stage(p_ref, prev_row, next_row, consts, out_refs, r0, PREP_SUB, state)
        _alternate(chain_stage, vector_stage)
        chain_stage = _prep_chain_stage(state, out_refs, r0, PREP_SUB)
    _alternate(chain_stage, None)


def _alternate(first, second):
    live = [g for g in (first, second) if g is not None]
    while live:
        for g in list(live):
            try:
                next(g)
            except StopIteration:
                live.remove(g)


def _prep_vector_stage(p_ref, prev_row, next_row, consts, out_refs, r0, ts, state):
    mu_ref, w0_ref, w2_ref, a0_ref, a2_ref, g2_ref, kk_ref, ka_ref, rk_ref, ones_ref = consts
    dir_refs = (out_refs[0:8], out_refs[8:16])
    v_ref, bonus_ref, gate_ref = out_refs[16:19]
    rows = slice(r0, r0 + ts)

    def shifted_mix(lo, hi):
        p = p_ref[0, rows, lo:hi]
        row = lax.broadcasted_iota(jnp.int32, p.shape, 0)
        prev = jnp.where(row == 0, prev_row[:, lo:hi], pltpu.roll(p, 1, 0))
        nxt = jnp.where(row == ts - 1, next_row[:, lo:hi], pltpu.roll(p, ts - 1, 0))
        return p + mu_ref[:, lo:hi] * (0.5 * (prev + nxt) - p)

    lora = shifted_mix(3 * RWKV_DIM, RWKV_COLS)
    wl, al, gl = lora[:, 0:LANES], lora[:, LANES:2 * LANES], lora[:, 2 * LANES:3 * LANES]
    yield
    r = shifted_mix(0, RWKV_DIM)
    yield
    k = shifted_mix(RWKV_DIM, 2 * RWKV_DIM)
    yield
    v = shifted_mix(2 * RWKV_DIM, 3 * RWKV_DIM)
    yield
    ones = ones_ref[...]

    u = w0_ref[...] + _dot(jnp.tanh(wl).astype(BF16), w2_ref[...])
    logw = (-float(np.exp(-0.5))) * jax.nn.sigmoid(u)
    asig = jax.nn.sigmoid(a0_ref[...] + _dot(al.astype(BF16), a2_ref[...]))
    yield
    kkr = k * kk_ref[...]
    kk = kkr * lax.rsqrt(jnp.maximum(_head_sum(kkr * kkr, ones, 1), 1e-24))
    k_a = ka_ref[...]
    kd = [k * (1.0 + (asig[:, d * RWKV_DIM:(d + 1) * RWKV_DIM] - 1.0) * k_a) for d in range(2)]
    yield
    bonus_ref[0, rows] = (_head_sum(r * rk_ref[...] * (kd[0] + kd[1]), ones, 1) * v).astype(BF16)
    gate_ref[0, rows] = _dot(jax.nn.sigmoid(gl).astype(BF16), g2_ref[...]).astype(BF16)
    v_ref[0, rows] = v.astype(BF16)
    yield

    rr = lax.broadcasted_iota(jnp.int32, (ts, ts), 0)
    cc = lax.broadcasted_iota(jnp.int32, (ts, ts), 1)
    same = (rr // CHUNK) == (cc // CHUNK)

    scaled = []
    for d in range(2):
        bp_ref, kp_ref, pt_ref = dir_refs[d][5:8]
        rt_ref = dir_refs[d][1]
        before = (cc <= rr) if d == 0 else (cc >= rr)
        lhs = jnp.where(same & before, 1.0, 0.0).astype(BF16)
        lw = logw[:, d * RWKV_DIM:(d + 1) * RWKV_DIM]
        cl = None
        for term in _split_bf16(lw, 2):
            y = _dot(lhs, term)
            cl = y if cl is None else cl + y
        yield
        ends = [c * CHUNK + (CHUNK - 1 if d == 0 else 0) for c in range(ts // CHUNK)]
        p_tot = [jnp.exp(cl[e:e + 1, :]) for e in ends]
        for c in range(ts // CHUNK):
            pt_ref[0, r0 // CHUNK + c] = p_tot[c]
        e_neg = jnp.exp(-cl)
        b_raw = kk * asig[:, d * RWKV_DIM:(d + 1) * RWKV_DIM]
        r_t = r * jnp.exp(cl)
        rt_ref[0, rows] = r_t.astype(BF16)
        yield
        e_out = e_neg * jnp.concatenate([jnp.broadcast_to(pc, (CHUNK, RWKV_DIM)) for pc in p_tot], axis=0)
        bp_ref[0, rows] = (b_raw * e_out).astype(BF16)
        kp_ref[0, rows] = (kd[d] * e_out).astype(BF16)
        yield
        scaled.append((-kk * jnp.exp(cl - lw), b_raw * e_neg, kd[d] * e_neg, r_t))
        yield
    state.update(scaled=scaled, v=v)


def _prep_chain_stage(state, out_refs, r0, ts):
    scaled, v = state["scaled"], state["v"]
    dir_refs = (out_refs[0:8], out_refs[8:16])
    tt = lax.broadcasted_iota(jnp.int32, (CHUNK, QUAD), 0)
    ss = lax.broadcasted_iota(jnp.int32, (CHUNK, QUAD), 1) % CHUNK
    eye = jnp.where(tt == ss, 1.0, 0.0)
    chains = [(d, c, qd) for d in range(2) for c in range(ts // CHUNK) for qd in range(RWKV_DIM // QUAD)]
    cut = lambda arr, c, qd: arr[c * CHUNK:(c + 1) * CHUNK, qd * QUAD:(qd + 1) * QUAD]
    strict = [ss < tt, ss > tt]
    incl = [ss <= tt, ss >= tt]
    at, x2 = {}, {}
    for ch in chains:
        d, c, qd = ch
        at[ch] = cut(scaled[d][0], c, qd)
        x2[ch] = jnp.concatenate([at[ch], cut(scaled[d][3], c, qd)], axis=0).astype(BF16)
    w_b = {ch: _dot_nt(x2[ch], _bd16(cut(scaled[ch[0]][1], ch[1], ch[2]))) for ch in chains}
    yield
    w_k = {ch: _dot_nt(x2[ch], _bd16(cut(scaled[ch[0]][2], ch[1], ch[2]))) for ch in chains}
    yield
    a_ab, gy = {}, {}
    for ch in chains:
        d, c, qd = ch
        a_ab[ch] = jnp.where(strict[d], w_b[ch][:CHUNK], 0.0)
        dir_refs[d][2][0, r0 + c * CHUNK:r0 + (c + 1) * CHUNK, qd * QUAD:(qd + 1) * QUAD] = (
            jnp.where(incl[d], w_b[ch][CHUNK:], 0.0).astype(BF16))
        lk = jnp.concatenate([jnp.where(strict[d], w_k[ch][:CHUNK], 0.0),
                              jnp.where(incl[d], w_k[ch][CHUNK:], 0.0)], axis=0)
        gy[ch] = _dot(lk.astype(BF16), _bd16(cut(v, c, qd)))
    yield
    tm = {ch: eye + a_ab[ch] for ch in chains}
    pw = {ch: _dot(a_ab[ch].astype(BF16), _bd16(a_ab[ch])) for ch in chains}
    yield
    for _ in range(4):
        both = {ch: _dot(jnp.concatenate([pw[ch], tm[ch]], axis=0).astype(BF16), _bd16(pw[ch])) for ch in chains}
        tm = {ch: tm[ch] + both[ch][CHUNK:] for ch in chains}
        pw = {ch: both[ch][:CHUNK] for ch in chains}
        yield
    tm = {ch: (tm[ch] + _dot(tm[ch].astype(BF16), _bd16(pw[ch]))).astype(BF16) for ch in chains}
    yield
    for ch in chains:
        d, c, qd = ch
        rs, ls = slice(r0 + c * CHUNK, r0 + (c + 1) * CHUNK), slice(qd * QUAD, (qd + 1) * QUAD)
        at_ref, _, _, uv_ref, yv_ref = dir_refs[d][0:5]
        uv_ref[0, rs, ls] = _dot(tm[ch], _bd16(gy[ch][:CHUNK])).astype(BF16)
        at_ref[0, rs, ls] = _dot(tm[ch], _bd16(at[ch])).astype(BF16)
        yv_ref[0, rs, ls] = gy[ch][CHUNK:].astype(BF16)


def _rwkv_prep(p, mu, w0, w2bd, a0, a2bd, g2, k_k, k_a, r_k, ones_bd):
    b, s, _ = p.shape
    ts = min(PREP_TILE, s)
    nt = s // ts
    h8 = ts // 8
    main = pl.BlockSpec((1, ts, RWKV_COLS), lambda bi, i: (bi, i, 0))
    halo_prev = pl.BlockSpec((1, 8, RWKV_COLS), lambda bi, i: (bi, jnp.maximum(i * h8 - 1, 0), 0))
    halo_next = pl.BlockSpec((1, 8, RWKV_COLS), lambda bi, i: (bi, jnp.minimum((i + 1) * h8, s // 8 - 1), 0))
    tok = pl.BlockSpec((1, ts, RWKV_DIM), lambda bi, i: (bi, i, 0))
    ptot = pl.BlockSpec((1, ts // CHUNK, 1, RWKV_DIM), lambda bi, i: (bi, i, 0, 0))
    tok16 = jax.ShapeDtypeStruct((b, s, RWKV_DIM), BF16)
    tok32 = jax.ShapeDtypeStruct((b, s, RWKV_DIM), F32)
    pt32 = jax.ShapeDtypeStruct((b, s // CHUNK, 1, RWKV_DIM), F32)
    per_dir_specs = [tok, tok, tok, tok, tok, tok, tok, ptot]
    per_dir_shapes = [tok16] * 7 + [pt32]
    consts = [mu, w0, w2bd, a0, a2bd, g2, k_k, k_a, r_k, ones_bd]
    return pl.pallas_call(
        functools.partial(_rwkv_prep_body, ts=ts),
        grid=(b, nt),
        in_specs=[main, halo_prev, halo_next] + [_resident(c.shape) for c in consts],
        out_specs=per_dir_specs * 2 + [tok, tok, tok],
        out_shape=per_dir_shapes * 2 + [tok16, tok16, tok16],
        compiler_params=_params("parallel", "parallel"),
        name="rwkv_prep",
    )(p, p, p, *consts)


def _rwkv_scan_body(*refs, nc, nb):
    fwd, bwd = refs[0:8], refs[8:16]
    vf_ref, vb_ref = refs[16:18]
    yf_ref, yb_ref = refs[18:20]
    st_ref = refs[20]
    step = pl.program_id(1)

    @pl.when(step == 0)
    def _():
        st_ref[...] = jnp.zeros_like(st_ref)

    rr = lax.broadcasted_iota(jnp.int32, (QUAD, QUAD), 0) // HEAD_DIM
    cc = lax.broadcasted_iota(jnp.int32, (QUAD, QUAD), 1) // HEAD_DIM
    diag = rr == cc
    dirs = ((fwd, vf_ref, yf_ref), (bwd, vb_ref, yb_ref))
    chains = [(bi, d, qd) for bi in range(nb) for d in range(2) for qd in range(RWKV_DIM // QUAD)]
    st = {ch: st_ref[ch] for ch in chains}
    for j in range(nc):
        z = {}
        for ch in chains:
            bi, d, qd = ch
            c = j if d == 0 else nc - 1 - j
            rs, ls = slice(c * CHUNK, (c + 1) * CHUNK), slice(qd * QUAD, (qd + 1) * QUAD)
            at_ref, rt_ref = dirs[d][0][0:2]
            x2 = jnp.concatenate([at_ref[bi, rs, ls], rt_ref[bi, rs, ls]], axis=0)
            z[ch] = _dot_nt(x2, st[ch].astype(BF16))
        for ch in chains:
            bi, d, qd = ch
            c = j if d == 0 else nc - 1 - j
            rs, ls = slice(c * CHUNK, (c + 1) * CHUNK), slice(qd * QUAD, (qd + 1) * QUAD)
            (_, _, mrb_ref, uv_ref, yv_ref, bp_ref, kp_ref, pt_ref), v_ref, y_ref = dirs[d]
            u = z[ch][:CHUNK] + uv_ref[bi, rs, ls]
            y = z[ch][CHUNK:] + _dot(mrb_ref[bi, rs, ls], _bd16(u)) + yv_ref[bi, rs, ls]
            y_ref[bi, rs, ls] = y.astype(BF16)
            uv2 = jnp.concatenate([u.astype(BF16), v_ref[bi, rs, ls]], axis=0)
            bk2 = jnp.concatenate([bp_ref[bi, rs, ls], kp_ref[bi, rs, ls]], axis=0)
            st[ch] = st[ch] * pt_ref[bi, c][:, ls] + jnp.where(diag, _dot_tn(uv2, bk2), 0.0)
    for ch in chains:
        st_ref[ch] = st[ch]


def _rwkv_scan(fwd_arrs, bwd_arrs, v16):
    b, s, _ = v16.shape
    nc = min(SCAN_CHUNKS, s // CHUNK)
    nb = SCAN_BATCH if b % SCAN_BATCH == 0 else 1
    rows = nc * CHUNK
    steps = s // rows
    f_tok = pl.BlockSpec((nb, rows, RWKV_DIM), lambda bi, i: (bi, i, 0))
    b_tok = pl.BlockSpec((nb, rows, RWKV_DIM), lambda bi, i: (bi, steps - 1 - i, 0))
    f_pt = pl.BlockSpec((nb, nc, 1, RWKV_DIM), lambda bi, i: (bi, i, 0, 0))
    b_pt = pl.BlockSpec((nb, nc, 1, RWKV_DIM), lambda bi, i: (bi, steps - 1 - i, 0, 0))
    y_shape = jax.ShapeDtypeStruct((b, s, RWKV_DIM), BF16)
    return pl.pallas_call(
        functools.partial(_rwkv_scan_body, nc=nc, nb=nb),
        grid=(b // nb, steps),
        in_specs=[f_tok] * 7 + [f_pt] + [b_tok] * 7 + [b_pt] + [f_tok, b_tok],
        out_specs=[f_tok, b_tok],
        out_shape=[y_shape, y_shape],
        scratch_shapes=[pltpu.VMEM((nb, 2, RWKV_DIM // QUAD, QUAD, QUAD), F32)],
        compiler_params=_params("parallel", "arbitrary"),
        name="rwkv_scan",
    )(*fwd_arrs, *bwd_arrs, v16, v16)


N_PAIRS = 4


def _pair_queries(q_ref, pair, rows=slice(None)):
    qp = q_ref[0, rows, pair * LANES:(pair + 1) * LANES]
    lane = lax.broadcasted_iota(jnp.int32, qp.shape, 1)
    zero = jnp.zeros_like(qp)
    return jnp.concatenate([jnp.where(lane < HEAD_DIM, qp, zero), jnp.where(lane >= HEAD_DIM, qp, zero)], axis=0)


def _store_pair(o_ref, pair, o, n, rows=slice(None)):
    lane = lax.broadcasted_iota(jnp.int32, (n, LANES), 1)
    o_ref[0, rows, pair * LANES:(pair + 1) * LANES] = jnp.where(lane < HEAD_DIM, o[:n], o[n:]).astype(BF16)


def _values_and_ones(vd):
    return jnp.concatenate([vd, jnp.ones_like(vd)], axis=1)


def _skewed_pairs(scores, softmax, finish, pieces=tuple(range(N_PAIRS))):
    sc, pr = {}, {}
    n = len(pieces)
    for step in range(n + 2):
        if step < n:
            sc[step] = scores(pieces[step])
        if 1 <= step <= n:
            pr[step - 1] = softmax(pieces[step - 1], sc.pop(step - 1))
        if step >= 2:
            finish(pieces[step - 2], pr.pop(step - 2))


def _ga_body(q_ref, kv_ref, o_ref, *, tq):
    blk = min(GA_Q_BLOCK, tq)
    rows = lambda j: slice(j * blk, (j + 1) * blk)

    def scores(piece):
        p, j = piece
        return _dot_nt(_pair_queries(q_ref, p, rows(j)), kv_ref[0, :, (p // 2) * LANES:(p // 2 + 1) * LANES])

    def softmax(piece, sc):
        return jnp.exp2(sc - jnp.max(sc, axis=-1, keepdims=True)).astype(BF16)

    def finish(piece, e):
        p, j = piece
        ov = _dot(e, _values_and_ones(kv_ref[0, :, (2 + p // 2) * LANES:(3 + p // 2) * LANES]))
        _store_pair(o_ref, p, ov[:, :LANES] / ov[:, LANES:], blk, rows(j))

    _skewed_pairs(scores, softmax, finish, [(p, j) for p in range(N_PAIRS) for j in range(tq // blk)])


def _global_attention(q, kv):
    b, s, w = q.shape
    tq = min(GA_Q_TILE, s)
    qs = pl.BlockSpec((1, tq, w), lambda bi, i: (bi, i, 0))
    kvs = pl.BlockSpec((1, s, w), lambda bi, i: (bi, 0, 0))
    return pl.pallas_call(
        functools.partial(_ga_body, tq=tq),
        grid=(b, s // tq),
        in_specs=[qs, kvs],
        out_specs=qs,
        out_shape=jax.ShapeDtypeStruct((b, s, w), BF16),
        compiler_params=_params("parallel", "arbitrary"),
        name="global_attn",
    )(q, kv)


def _wa_body(sink_ref, q_ref, kv_ref, *refs, blk, span, seq, nsub):
    band_refs, o_ref = refs[:nsub], refs[nsub]
    first = pl.program_id(1) * nsub
    starts = [pl.multiple_of(jnp.clip((first + j) * blk - WINDOW, 0, seq - span), WINDOW) for j in range(nsub)]
    valid = []
    for j in range(nsub):
        band = band_refs[j][0] > 0.5
        valid.append(jnp.concatenate([band, band], axis=0))

    def scores(piece):
        p, j = piece
        kd = kv_ref[0, pl.ds(starts[j], span), (p // 2) * LANES:(p // 2 + 1) * LANES]
        sc = _dot_nt(_pair_queries(q_ref, p, slice(j * blk, (j + 1) * blk)), kd)
        return jnp.where(valid[j], sc, NEG_INF)

    def softmax(piece, sc):
        p, _ = piece
        e, sink_e = [], []
        for h in range(2):
            sc_h = sc[h * blk:(h + 1) * blk]
            sink = sink_ref[2 * p + h] * LOG2E
            m = jnp.maximum(jnp.max(sc_h, axis=-1, keepdims=True), sink)
            e.append(jnp.exp2(sc_h - m).astype(BF16))
            sink_e.append(jnp.exp2(sink - m))
        return jnp.concatenate(e, axis=0), jnp.concatenate(sink_e, axis=0)

    def finish(piece, es):
        p, j = piece
        e, sink_e = es
        vd = kv_ref[0, pl.ds(starts[j], span), (2 + p // 2) * LANES:(3 + p // 2) * LANES]
        ov = _dot(e, _values_and_ones(vd))
        _store_pair(o_ref, p, ov[:, :LANES] / (ov[:, LANES:] + sink_e), blk, slice(j * blk, (j + 1) * blk))

    _skewed_pairs(scores, softmax, finish, [(p, j) for p in range(N_PAIRS) for j in range(nsub)])


def _window_attention(q, kv, sink):
    b, s, w = q.shape
    blk = min(WA_Q_BLOCK, s)
    span = min(blk + 2 * WINDOW, s)
    nblk = s // blk
    nsub = WA_SUB_BLOCKS if nblk % WA_SUB_BLOCKS == 0 else 1
    offsets = sorted({int(np.clip(i * blk - WINDOW, 0, s - span)) - i * blk for i in range(nblk)}, reverse=True)
    rel = jnp.arange(span)[None, :] - jnp.arange(blk)[:, None]
    band = jnp.stack([(jnp.abs(rel + off) <= WINDOW).astype(F32) for off in offsets])
    which = lambda i: jnp.where(i == 0, 0, jnp.where(i == nblk - 1, len(offsets) - 1, min(1, len(offsets) - 1)))
    qs = pl.BlockSpec((1, nsub * blk, w), lambda bi, i, *_: (bi, i, 0))
    kvs = pl.BlockSpec((1, s, w), lambda bi, i, *_: (bi, 0, 0))
    bands = [pl.BlockSpec((1, blk, span), functools.partial(lambda j, bi, i, *_: (which(i * nsub + j), 0, 0), j))
             for j in range(nsub)]
    return pl.pallas_call(
        functools.partial(_wa_body, blk=blk, span=span, seq=s, nsub=nsub),
        grid_spec=pltpu.PrefetchScalarGridSpec(
            num_scalar_prefetch=1, grid=(b, nblk // nsub), in_specs=[qs, kvs] + bands, out_specs=qs),
        out_shape=jax.ShapeDtypeStruct((b, s, w), BF16),
        compiler_params=_params("parallel", "arbitrary"),
        name="window_attn",
    )(sink, q, kv, *([band] * nsub))


def _merge_body(x_ref, g_ref, wg_ref, wf_ref, wb_ref, bonus_ref, rgate_ref, lg_ref, lb_ref, ones_ref,
                yb_ref, yc_ref, woa_ref, wob_ref, woc_ref, wout_ref, o_ref):
    ones = ones_ref[...]

    def prepare(rs):
        h = _rmsnorm(x_ref[rs, :], g_ref[...]).astype(BF16)
        wkv = wf_ref[rs, :].astype(F32) + wb_ref[rs, :].astype(F32)
        dev = wkv - _head_sum(wkv, ones, 2) * (1.0 / HEAD_DIM)
        var = _head_sum(dev * dev, ones, 1) * (1.0 / HEAD_DIM)
        y_a = dev * lax.rsqrt(var + LNX_EPS) * lg_ref[...] + lb_ref[...] + bonus_ref[rs, :].astype(F32)
        return h, (y_a * rgate_ref[rs, :].astype(F32)).astype(BF16)

    def gated_sum(rs, h, y_a):
        gates = jax.nn.sigmoid(_dot(h, wg_ref[:, MIX_COLS:]))
        merged = None
        for j, (y, wo_ref) in enumerate(((y_a, woa_ref), (yb_ref[rs, :], wob_ref), (yc_ref[rs, :], woc_ref))):
            term = gates[:, j * D_MODEL:(j + 1) * D_MODEL] * _dot(y, wo_ref[...])
            merged = term if merged is None else merged + term
        return merged.astype(BF16)

    parts = _row_parts(x_ref.shape[0])
    prepared = [prepare(rs) for rs in parts]
    merged = [gated_sum(rs, *pr) for rs, pr in zip(parts, prepared)]
    for rs, m in zip(parts, merged):
        o_ref[rs, :] = x_ref[rs, :] + _dot(m, wout_ref[...])


def _merge(x, g, wg, layer, wkv_f, wkv_b, bonus, rgate, lnx_g, lnx_b, ones_bd, yb, yc, woa, wob, woc, wout):
    t, d = x.shape
    tm = min(TOKEN_TILE, t)
    row = lambda w_: pl.BlockSpec((tm, w_), lambda i: (i, 0))
    half = row(RWKV_DIM)
    (wg_spec, wg), (woa_spec, woa), (wob_spec, wob), (woc_spec, woc), (wout_spec, wout) = (
        _layer_weight(w, layer) for w in (wg, woa, wob, woc, wout))
    return pl.pallas_call(
        _merge_body,
        grid=(t // tm,),
        in_specs=[row(d), _resident((1, d)), wg_spec, half, half, half, half,
                  _resident(lnx_g.shape), _resident(lnx_b.shape), _resident(ones_bd.shape), half, half,
                  woa_spec, wob_spec, woc_spec, wout_spec],
        out_specs=row(d),
        out_shape=jax.ShapeDtypeStruct((t, d), F32),
        compiler_params=_params("parallel"),
        name="merge",
    )(x, g, wg, wkv_f, wkv_b, bonus, rgate, lnx_g, lnx_b, ones_bd, yb, yc, woa, wob, woc, wout)


def _rope_tables(seq):
    t = jnp.arange(seq)

    def angles(pos, dim):
        inv_freq = ROPE_THETA ** (-jnp.arange(0, dim, 2, dtype=F32) / dim)
        return pos.astype(F32)[:, None] * inv_freq[None, :]

    ang_row, ang_col = angles(t // GRID_W, HEAD_DIM // 2), angles(t % GRID_W, HEAD_DIM // 2)
    ang_1d = angles(t, HEAD_DIM)
    ga_cos = jnp.concatenate([jnp.cos(ang_row)] * 2 + [jnp.cos(ang_col)] * 2, axis=1)
    ga_sin = jnp.concatenate([-jnp.sin(ang_row), jnp.sin(ang_row), -jnp.sin(ang_col), jnp.sin(ang_col)], axis=1)
    wa_cos = jnp.concatenate([jnp.cos(ang_1d)] * 2, axis=1)
    wa_sin = jnp.concatenate([-jnp.sin(ang_1d), jnp.sin(ang_1d)], axis=1)
    return tuple(jnp.concatenate([a, a], axis=1) for a in (ga_cos, ga_sin, wa_cos, wa_sin))


def _two_way(w):
    z = jnp.zeros_like(w[0])
    return jnp.concatenate([jnp.concatenate([w[0], z], axis=1), jnp.concatenate([z, w[1]], axis=1)], axis=0)


def kernel(x, ffn1_norm, ffn1_w_up, ffn1_w_down, mix_norm, w_in, rwkv_mu, rwkv_w0, rwkv_w2, rwkv_a0, rwkv_a2, rwkv_g2, rwkv_k_k, rwkv_k_a, rwkv_r_k, rwkv_lnx_g, rwkv_lnx_b, ga_q_norm, ga_k_norm, wa_sink, w_o_rwkv, w_o_ga, w_o_wa, w_out, ffn2_norm, ffn2_w_up, ffn2_w_down, final_norm):
    bsz, seq, dm = x.shape
    depth = w_in.shape[0]
    tokens = bsz * seq
    tabs = _rope_tables(seq)
    idx = jnp.arange(RWKV_DIM)
    ones_bd = (idx[:, None] // HEAD_DIM == idx[None, :] // HEAD_DIM).astype(BF16)
    row1 = lambda a: a.reshape(1, -1)
    final_g = row1(final_norm)

    f1_up, f1_down, f2_up, f2_down, w_in16, wo_a, wo_g, wo_w, w_out16 = (
        _to_bf16(w) for w in (ffn1_w_up, ffn1_w_down, ffn2_w_up, ffn2_w_down, w_in, w_o_rwkv, w_o_ga, w_o_wa, w_out))

    xt = x.reshape(tokens, dm)
    for l in range(depth):
        xt = _ffn(xt, row1(ffn1_norm[l]), f1_up, f1_down, l, final_g, False)

        p_rwkv, ga_q, ga_kv, wa_q, wa_kv = _inproj(
            xt, row1(mix_norm[l]), w_in16, l, ones_bd, row1(jnp.tile(ga_q_norm[l], 8)),
            row1(jnp.tile(ga_k_norm[l], 8)), tabs, seq)

        outs = _rwkv_prep(
            p_rwkv.reshape(bsz, seq, RWKV_COLS), row1(rwkv_mu[l]), row1(rwkv_w0[l]), _two_way(rwkv_w2[l]).astype(BF16),
            row1(rwkv_a0[l]), _two_way(rwkv_a2[l]).astype(BF16), rwkv_g2[l].astype(BF16), row1(rwkv_k_k[l]),
            row1(rwkv_k_a[l]), row1(rwkv_r_k[l]), ones_bd)
        v16, bonus, gate = outs[16:19]
        y_f, y_b = _rwkv_scan(outs[0:8], outs[8:16], v16)
        flat = lambda a: a.reshape(tokens, RWKV_DIM)

        per_seq = lambda a: a.reshape(bsz, seq, ATT_Q)
        y_g = _global_attention(per_seq(ga_q), per_seq(ga_kv))
        y_w = _window_attention(per_seq(wa_q), per_seq(wa_kv), wa_sink[l])

        xt = _merge(xt, row1(mix_norm[l]), w_in16, l, flat(y_f), flat(y_b), flat(bonus), flat(gate),
                    row1(rwkv_lnx_g[l]), row1(rwkv_lnx_b[l]), ones_bd, flat(y_g), flat(y_w),
                    wo_a, wo_g, wo_w, w_out16)

        xt = _ffn(xt, row1(ffn2_norm[l]), f2_up, f2_down, l, final_g, l == depth - 1)
    return xt.reshape(bsz, seq, dm)
```
